```python
import math
import jax
import jax.numpy as jnp
from jax import lax
import numpy as np

D_MODEL = 2048
BATCH = 16
SEQ = 2048
DEPTH = 4

ATTN_WIDTH = D_MODEL // 2
ATTN_HEADS = 8
ATTN_HEAD_DIM = ATTN_WIDTH // ATTN_HEADS
MLSTM_WIDTH = D_MODEL - ATTN_WIDTH
MLSTM_HEADS = 4
MLSTM_HEAD_DIM = MLSTM_WIDTH // MLSTM_HEADS
MIX_WIDTH = ATTN_WIDTH + MLSTM_WIDTH
N_MLSTM_GATES = 4 * MLSTM_HEADS
MIX_COLS = 3 * ATTN_WIDTH + 4 * MLSTM_WIDTH + N_MLSTM_GATES

DILATED_BRANCHES = ((128, 1), (512, 4), (2048, 16))
BAND_BLOCK = 64
REL_BUCKETS = 32
REL_MAX_DIST = 1024
NEG_INF = -1e30

MLSTM_CHUNK = 64
CONV_WIDTH = 5

D_FF = 5632
N_EXPERTS = 8
TOP_K = 2
D_FF_EXPERT = D_FF // 2

DEEPNORM_ALPHA = (2 * DEPTH) ** 0.25
DEEPNORM_BETA = (8 * DEPTH) ** -0.25
LN_EPS = 1e-5

kernel_name = 'hybrid_dilated_attn_mlstm_moe_deepnorm_encoder'


def layer_norm(x, g, b):
    xf = x.astype(jnp.float32)
    mu = xf.mean(-1, keepdims=True)
    var = jnp.square(xf - mu).mean(-1, keepdims=True)
    return ((xf - mu) * lax.rsqrt(var + LN_EPS) * g.astype(jnp.float32)
            + b.astype(jnp.float32)).astype(x.dtype)


def t5_bucket(rel):
    half = REL_BUCKETS // 2
    max_exact = half // 2
    n = jnp.abs(rel)
    large = max_exact + (jnp.log(jnp.maximum(n, 1).astype(jnp.float32) / max_exact)
                         / math.log(REL_MAX_DIST / max_exact)
                         * (half - max_exact)).astype(jnp.int32)
    large = jnp.minimum(large, half - 1)
    return jnp.where(rel > 0, half, 0) + jnp.where(n < max_exact, n, large)


def branch_bias(table, dilation):
    qi = jnp.arange(BAND_BLOCK)[:, None]
    t = jnp.arange(3 * BAND_BLOCK)[None, :]
    rel = (t - BAND_BLOCK - qi) * dilation
    return table[t5_bucket(rel)].transpose(2, 0, 1).astype(jnp.float32)


def dilated_branch(q, k, v, bias, window, dilation):
    B, H, S, Dh = q.shape
    half = window // (2 * dilation)
    L = S // dilation
    nb = -(-L // BAND_BLOCK)
    Lp = nb * BAND_BLOCK

    def by_residue(t):
        t = t.reshape(B, H, L, dilation, Dh).transpose(0, 1, 3, 2, 4)
        return jnp.pad(t, ((0, 0), (0, 0), (0, 0), (0, Lp - L), (0, 0)))

    def band(t):
        t = jnp.pad(by_residue(t), ((0, 0), (0, 0), (0, 0), (BAND_BLOCK, BAND_BLOCK), (0, 0)))
        t = t.reshape(B, H, dilation, nb + 2, BAND_BLOCK, Dh)
        return jnp.concatenate([t[:, :, :, :-2], t[:, :, :, 1:-1], t[:, :, :, 2:]], axis=4)

    qb = by_residue(q).reshape(B, H, dilation, nb, BAND_BLOCK, Dh)
    kb = band(k)
    vb = band(v)
    logits = (jnp.einsum('bhrnqc,bhrnkc->bhrnqk', qb, kb).astype(jnp.float32) * (Dh ** -0.5)
              + bias[None, :, None, None])
    qi = jnp.arange(BAND_BLOCK)[:, None]
    t = jnp.arange(3 * BAND_BLOCK)[None, :]
    kpos = jnp.arange(nb)[:, None, None] * BAND_BLOCK - BAND_BLOCK + t[None]
    valid = (jnp.abs(t - BAND_BLOCK - qi) <= half)[None] & (kpos >= 0) & (kpos < L)
    logits = jnp.where(valid, logits, NEG_INF)
    m = logits.max(-1, keepdims=True)
    p = jnp.exp(logits - m)
    s = p.sum(-1, keepdims=True)
    o = jnp.einsum('bhrnqk,bhrnkc->bhrnqc', p, vb.astype(jnp.float32)) / s
    lse = (m + jnp.log(s))[..., 0]
    o = o.reshape(B, H, dilation, Lp, Dh)[:, :, :, :L].transpose(0, 1, 3, 2, 4).reshape(B, H, S, Dh)
    lse = lse.reshape(B, H, dilation, Lp)[:, :, :, :L].transpose(0, 1, 3, 2).reshape(B, H, S)
    return o, lse


def dilated_attention(q, k, v, biases):
    outs = []
    lses = []
    for bias, (window, dilation) in zip(biases, DILATED_BRANCHES):
        o, lse = dilated_branch(q, k, v, bias, window, dilation)
        outs.append(o)
        lses.append(lse)
    wts = jax.nn.softmax(jnp.stack(lses), axis=0)
    return jnp.einsum('gbhs,gbhsc->bhsc', wts, jnp.stack(outs))


def mlstm_chunkwise(q, k, v, log_i, log_f):
    B, H, S, Dh = q.shape
    nc = S // MLSTM_CHUNK

    def chunks(t):
        return jnp.moveaxis(t.reshape((B, H, nc, MLSTM_CHUNK) + t.shape[3:]), 2, 0)

    lower = jnp.tril(jnp.ones((MLSTM_CHUNK, MLSTM_CHUNK), dtype=bool))

    def step(carry, inp):
        C, n, m = carry
        qc, kc, vc, li, lf = inp
        b = jnp.cumsum(lf, axis=-1)
        Dm = jnp.where(lower, b[..., :, None] - b[..., None, :] + li[..., None, :], NEG_INF)
        inter = b + m[..., None]
        mt = jnp.maximum(inter, Dm.max(-1))
        w_intra = jnp.exp(Dm - mt[..., None])
        w_inter = jnp.exp(inter - mt)
        sc = jnp.einsum('bhtc,bhsc->bhts', qc, kc) * w_intra
        num = (jnp.einsum('bhts,bhsc->bhtc', sc, vc)
               + w_inter[..., None] * jnp.einsum('bhvk,bhtk->bhtv', C, qc))
        den = sc.sum(-1) + w_inter * jnp.einsum('bhk,bhtk->bht', n, qc)
        h = num / jnp.maximum(jnp.abs(den), jnp.exp(-mt))[..., None]
        bl = b[..., -1]
        g = bl[..., None] - b + li
        m_new = jnp.maximum(bl + m, g.max(-1))
        decay = jnp.exp(bl + m - m_new)
        wk = jnp.exp(g - m_new[..., None])
        C_new = decay[..., None, None] * C + jnp.einsum('bhsv,bhsk->bhvk', vc * wk[..., None], kc)
        n_new = decay[..., None] * n + jnp.einsum('bhs,bhsk->bhk', wk, kc)
        return (C_new, n_new, m_new), h

    init = (jnp.zeros((B, H, Dh, Dh), jnp.float32), jnp.zeros((B, H, Dh), jnp.float32),
            jnp.zeros((B, H), jnp.float32))
    _, hs = lax.scan(step, init, (chunks(q), chunks(k), chunks(v), chunks(log_i), chunks(log_f)))
    return jnp.moveaxis(hs, 0, 2).reshape(B, H, S, Dh)


def centred_conv(x, w):
    C = x.shape[-1]
    return lax.conv_general_dilated(
        x, w[:, None, :].astype(x.dtype), window_strides=(1,),
        padding=[(CONV_WIDTH // 2, CONV_WIDTH // 2)],
        dimension_numbers=('NWC', 'WIO', 'NWC'), feature_group_count=C)


def hybrid_mixer(x, w_in, w_out, conv_w, gate_b, biases):
    B, S, _ = x.shape
    proj = x @ w_in
    cuts = [int(c) for c in np.cumsum([ATTN_WIDTH] * 3 + [MLSTM_WIDTH] * 4)]
    aq, ak, av, mq, mk, mv, mo, mg = jnp.split(proj, cuts, axis=-1)

    def heads(t, nh, dh):
        return t.reshape(B, S, nh, dh).transpose(0, 2, 1, 3)

    attn = dilated_attention(heads(aq, ATTN_HEADS, ATTN_HEAD_DIM), heads(ak, ATTN_HEADS, ATTN_HEAD_DIM),
                             heads(av, ATTN_HEADS, ATTN_HEAD_DIM), biases)
    attn = attn.transpose(0, 2, 1, 3).reshape(B, S, ATTN_WIDTH)

    mqk = jax.nn.silu(centred_conv(jnp.concatenate([mq, mk], axis=-1), conv_w))
    mq, mk = jnp.split(mqk, 2, axis=-1)
    q = heads(mq, MLSTM_HEADS, MLSTM_HEAD_DIM).astype(jnp.float32)
    k = heads(mk, MLSTM_HEADS, MLSTM_HEAD_DIM).astype(jnp.float32) * (MLSTM_HEAD_DIM ** -0.5)
    v = heads(mv, MLSTM_HEADS, MLSTM_HEAD_DIM).astype(jnp.float32)
    gates = (mg + gate_b).astype(jnp.float32).reshape(B, S, 4, MLSTM_HEADS).transpose(2, 0, 3, 1)
    i_fw, f_fw, i_bw, f_bw = gates[0], gates[1], gates[2], gates[3]
    h_fw = mlstm_chunkwise(q, k, v, i_fw, jax.nn.log_sigmoid(f_fw))
    fl = lambda t: jnp.flip(t, axis=2)
    h_bw = fl(mlstm_chunkwise(fl(q), fl(k), fl(v), fl(i_bw), fl(jax.nn.log_sigmoid(f_bw))))
    h = (h_fw + h_bw).transpose(0, 2, 1, 3).reshape(B, S, MLSTM_WIDTH)
    h = jax.nn.sigmoid(mo.astype(jnp.float32)) * h

    mixed = jnp.concatenate([attn.astype(x.dtype), h.astype(x.dtype)], axis=-1)
    return mixed @ w_out


def swiglu(x, w1, w3, w2):
    return (jax.nn.silu(x @ w1) * (x @ w3)) @ w2


def moe_swiglu(x, wr, br, w1, w3, w2):
    B, S, D = x.shape
    xt = x.reshape(B * S, D)
    logits = (xt @ wr + br).astype(jnp.float32)
    top_v, top_i = lax.top_k(logits, TOP_K)
    probs = jax.nn.softmax(top_v, axis=-1)
    gates = jnp.einsum('tk,tke->te', probs, jax.nn.one_hot(top_i, N_EXPERTS, dtype=jnp.float32))
    out = jnp.zeros((B * S, D), jnp.float32)
    for e in range(N_EXPERTS):
        out = out + gates[:, e:e + 1] * swiglu(xt, w1[e], w3[e], w2[e]).astype(jnp.float32)
    return out.astype(x.dtype).reshape(B, S, D)


def setup_inputs(seed: int = 0) -> dict:
    key = jax.random.key(seed)
    ks = jax.random.split(key, 17)
    nrm = jax.random.normal
    n_dense = (DEPTH + 1) // 2
    n_moe = DEPTH // 2
    x = nrm(ks[0], (BATCH, SEQ, D_MODEL), jnp.float32)
    col_scale = jnp.concatenate([
        jnp.ones((2 * ATTN_WIDTH,)), jnp.full((ATTN_WIDTH,), DEEPNORM_BETA),
        jnp.ones((2 * MLSTM_WIDTH,)), jnp.full((MLSTM_WIDTH,), DEEPNORM_BETA),
        jnp.ones((MLSTM_WIDTH + N_MLSTM_GATES,))]).astype(jnp.float32)
    w_in = nrm(ks[1], (DEPTH, D_MODEL, MIX_COLS), jnp.float32) * (D_MODEL ** -0.5) * col_scale
    w_out = nrm(ks[2], (DEPTH, MIX_WIDTH, D_MODEL), jnp.float32) * (MIX_WIDTH ** -0.5) * DEEPNORM_BETA
    conv_w = nrm(ks[3], (DEPTH, CONV_WIDTH, 2 * MLSTM_WIDTH), jnp.float32) * (CONV_WIDTH ** -0.5)
    i_bias = 0.1 * nrm(ks[4], (DEPTH, 2, MLSTM_HEADS), jnp.float32)
    f_bias = jnp.linspace(3.0, 6.0, MLSTM_HEADS, dtype=jnp.float32) + 0.1 * nrm(ks[5], (DEPTH, 2, MLSTM_HEADS), jnp.float32)
    gate_b = jnp.stack([i_bias[:, 0], f_bias[:, 0], i_bias[:, 1], f_bias[:, 1]], axis=1).reshape(DEPTH, N_MLSTM_GATES)
    rpb_table = 0.2 * nrm(ks[6], (REL_BUCKETS, ATTN_HEADS), jnp.float32)
    ln_g = 1.0 + 0.02 * nrm(ks[7], (DEPTH, 2, D_MODEL), jnp.float32)
    ln_b = 0.02 * nrm(ks[8], (DEPTH, 2, D_MODEL), jnp.float32)
    dense_w1 = nrm(ks[9], (n_dense, D_MODEL, D_FF), jnp.float32) * (D_MODEL ** -0.5)
    dense_w3 = nrm(ks[10], (n_dense, D_MODEL, D_FF), jnp.float32) * (D_MODEL ** -0.5)
    dense_w2 = nrm(ks[11], (n_dense, D_FF, D_MODEL), jnp.float32) * (D_FF ** -0.5) * DEEPNORM_BETA
    router_w = nrm(ks[12], (n_moe, D_MODEL, N_EXPERTS), jnp.float32) * (D_MODEL ** -0.5)
    router_b = 0.01 * nrm(ks[13], (n_moe, N_EXPERTS), jnp.float32)
    moe_w1 = nrm(ks[14], (n_moe, N_EXPERTS, D_MODEL, D_FF_EXPERT), jnp.float32) * (D_MODEL ** -0.5)
    moe_w3 = nrm(ks[15], (n_moe, N_EXPERTS, D_MODEL, D_FF_EXPERT), jnp.float32) * (D_MODEL ** -0.5)
    moe_w2 = nrm(ks[16], (n_moe, N_EXPERTS, D_FF_EXPERT, D_MODEL), jnp.float32) * (D_FF_EXPERT ** -0.5) * DEEPNORM_BETA
    return {'x': x, 'w_in': w_in, 'w_out': w_out, 'conv_w': conv_w, 'gate_b': gate_b,
            'rpb_table': rpb_table, 'ln_g': ln_g, 'ln_b': ln_b,
            'dense_w1': dense_w1, 'dense_w3': dense_w3, 'dense_w2': dense_w2,
            'router_w': router_w, 'router_b': router_b,
            'moe_w1': moe_w1, 'moe_w3': moe_w3, 'moe_w2': moe_w2}


def reference(x, w_in, w_out, conv_w, gate_b, rpb_table, ln_g, ln_b,
              dense_w1, dense_w3, dense_w2, router_w, router_b, moe_w1, moe_w3, moe_w2):
    biases = [branch_bias(rpb_table, d) for _, d in DILATED_BRANCHES]
    for l in range(DEPTH):
        h = hybrid_mixer(x, w_in[l], w_out[l], conv_w[l], gate_b[l], biases)
        x = layer_norm(DEEPNORM_ALPHA * x + h, ln_g[l, 0], ln_b[l, 0])
        j = l // 2
        if l % 2 == 0:
            f = swiglu(x, dense_w1[j], dense_w3[j], dense_w2[j])
        else:
            f = moe_swiglu(x, router_w[j], router_b[j], moe_w1[j], moe_w3[j], moe_w2[j])
        x = layer_norm(DEEPNORM_ALPHA * x + f, ln_g[l, 1], ln_b[l, 1])
    return x
```

```python
import functools
import math

import jax
import jax.numpy as jnp
from jax import lax
from jax.experimental import pallas as pl
from jax.experimental.pallas import tpu as pltpu

F32 = jnp.float32
BF16 = jnp.bfloat16

ATTN_HEADS = 8
ATTN_HEAD_DIM = 128
MLSTM_HEADS = 4
MLSTM_HEAD_DIM = 256
DILATED_BRANCHES = ((128, 1), (512, 4), (2048, 16))
BAND_HALF = 64
REL_BUCKETS = 32
REL_MAX_DIST = 1024
NEG_INF = -1e30
CONV_WIDTH = 5
N_EXPERTS = 8
LN_EPS = 1e-5

V7X_VMEM_BYTES = 64 * 1024 * 1024
VMEM_LIMIT_BYTES = V7X_VMEM_BYTES - 8 * 1024 * 1024
LANES = 128

ATTN_Q_BLOCK = 128
MLSTM_CHUNK = 256


def _params(*semantics):
    return pltpu.CompilerParams(dimension_semantics=semantics,
                                vmem_limit_bytes=VMEM_LIMIT_BYTES)


def _mm_kernel(a_ref, b_ref, o_ref):
    o_ref[...] = jnp.dot(a_ref[...], b_ref[...],
                         preferred_element_type=F32).astype(o_ref.dtype)


def matmul(a, b, *, tm, tn, out_dtype, name):
    M, K = a.shape
    N = b.shape[1]
    assert M % tm == 0 and N % tn == 0
    return pl.pallas_call(
        _mm_kernel,
        grid=(M // tm, N // tn),
        in_specs=[pl.BlockSpec((tm, K), lambda i, j: (i, 0)),
                  pl.BlockSpec((K, tn), lambda i, j: (0, j))],
        out_specs=pl.BlockSpec((tm, tn), lambda i, j: (i, j)),
        out_shape=jax.ShapeDtypeStruct((M, N), out_dtype),
        compiler_params=_params("parallel", "parallel"),
        name=name,
    )(a, b)


def _swiglu_kernel(a_ref, w1_ref, w3_ref, o_ref):
    a = a_ref[...]
    h1 = jnp.dot(a, w1_ref[...], preferred_element_type=F32)
    h3 = jnp.dot(a, w3_ref[...], preferred_element_type=F32)
    o_ref[...] = (h1 * jax.nn.sigmoid(h1) * h3).astype(o_ref.dtype)


def swiglu_matmul(a, w1, w3, *, tm, tf, name):
    M, K = a.shape
    F = w1.shape[1]
    assert M % tm == 0 and F % tf == 0
    return pl.pallas_call(
        _swiglu_kernel,
        grid=(M // tm, F // tf),
        in_specs=[pl.BlockSpec((tm, K), lambda i, j: (i, 0)),
                  pl.BlockSpec((K, tf), lambda i, j: (0, j)),
                  pl.BlockSpec((K, tf), lambda i, j: (0, j))],
        out_specs=pl.BlockSpec((tm, tf), lambda i, j: (i, j)),
        out_shape=jax.ShapeDtypeStruct((M, F), BF16),
        compiler_params=_params("parallel", "parallel"),
        name=name,
    )(a, w1, w3)


def _mm_epilogue_kernel(*refs, nk, gate_col, has_prev, has_ln, alpha):
    it = iter(refs)
    a_ref, b_ref = next(it), next(it)
    gate_ref = next(it) if gate_col is not None else None
    prev_ref = next(it) if has_prev else None
    if has_ln:
        res_ref, lng_ref, lnb_ref = next(it), next(it), next(it)
    o_ref = next(it)
    ob_ref = next(it) if has_ln else None
    acc_ref = next(it)

    k = pl.program_id(1)
    part = jnp.dot(a_ref[...], b_ref[...], preferred_element_type=F32)

    @pl.when(k == 0)
    def _():
        acc_ref[...] = part

    @pl.when(k > 0)
    def _():
        acc_ref[...] += part

    @pl.when(k == nk - 1)
    def _():
        y = acc_ref[...]
        if gate_ref is not None:
            y = y * gate_ref[:, gate_col:gate_col + 1]
        if prev_ref is not None:
            y = y + prev_ref[...]
        if has_ln:
            z = alpha * res_ref[...] + y
            mu = jnp.mean(z, axis=-1, keepdims=True)
            zc = z - mu
            var = jnp.mean(zc * zc, axis=-1, keepdims=True)
            out = zc * lax.rsqrt(var + LN_EPS) * lng_ref[...] + lnb_ref[...]
            o_ref[...] = out
            ob_ref[...] = out.astype(BF16)
        else:
            o_ref[...] = y


def matmul_epilogue(a, b, *, tm, tk, name, gates=None, gate_col=None, prev=None,
                    ln=None, alpha=None):
    M, K = a.shape
    N = b.shape[1]
    assert M % tm == 0 and K % tk == 0
    nk = K // tk
    row = lambda i, k: (i, 0)
    inputs = [a, b]
    in_specs = [pl.BlockSpec((tm, tk), lambda i, k: (i, k)),
                pl.BlockSpec((tk, N), lambda i, k: (k, 0))]
    if gates is not None:
        inputs.append(gates)
        in_specs.append(pl.BlockSpec((tm, gates.shape[1]), row))
    if prev is not None:
        inputs.append(prev)
        in_specs.append(pl.BlockSpec((tm, N), row))
    if ln is not None:
        res, g, beta = ln
        inputs += [res, g.reshape(1, N), beta.reshape(1, N)]
        in_specs += [pl.BlockSpec((tm, N), row),
                     pl.BlockSpec((1, N), lambda i, k: (0, 0)),
                     pl.BlockSpec((1, N), lambda i, k: (0, 0))]
    out_shape = [jax.ShapeDtypeStruct((M, N), F32)]
    out_specs = [pl.BlockSpec((tm, N), row)]
    if ln is not None:
        out_shape.append(jax.ShapeDtypeStruct((M, N), BF16))
        out_specs.append(pl.BlockSpec((tm, N), row))
    kern = functools.partial(_mm_epilogue_kernel, nk=nk,
                             gate_col=gate_col if gates is not None else None,
                             has_prev=prev is not None, has_ln=ln is not None, alpha=alpha)
    outs = pl.pallas_call(
        kern,
        grid=(M // tm, nk),
        in_specs=in_specs,
        out_specs=out_specs,
        out_shape=out_shape,
        scratch_shapes=[pltpu.VMEM((tm, N), F32)],
        compiler_params=_params("parallel", "arbitrary"),
        name=name,
    )(*inputs)
    return outs if ln is not None else outs[0]


def _t5_bucket(rel):
    half = REL_BUCKETS // 2
    max_exact = half // 2
    n = jnp.abs(rel)
    large = max_exact + (jnp.log(jnp.maximum(n, 1).astype(F32) / max_exact)
                         / math.log(REL_MAX_DIST / max_exact)
                         * (half - max_exact)).astype(jnp.int32)
    large = jnp.minimum(large, half - 1)
    return jnp.where(rel > 0, half, 0) + jnp.where(n < max_exact, n, large)


_ATTN_WINDOW_DELTAS = (0, -BAND_HALF, -2 * BAND_HALF)


def _attn_bias_tiles(table):
    qi = jnp.arange(ATTN_Q_BLOCK)[:, None]
    kj = jnp.arange(2 * ATTN_Q_BLOCK)[None, :]
    tiles = []
    for window, dilation in DILATED_BRANCHES:
        assert window // (2 * dilation) == BAND_HALF
        per_case = []
        for delta in _ATTN_WINDOW_DELTAS:
            m = kj + delta - qi
            bias = table[_t5_bucket(m * dilation)].astype(F32)
            bias = jnp.where((jnp.abs(m) <= BAND_HALF)[..., None], bias, NEG_INF)
            per_case.append(bias.transpose(2, 0, 1))
        tiles.append(jnp.stack(per_case))
    return jnp.stack(tiles)


def _band_block(q, k, v, bias):
    s = lax.dot_general(q, k, (((1,), (1,)), ((), ())), preferred_element_type=F32)
    s = s * (ATTN_HEAD_DIM ** -0.5) + bias
    m = jnp.max(s, axis=-1, keepdims=True)
    p = jnp.exp(s - m)
    l = jnp.sum(p, axis=-1, keepdims=True)
    o = jnp.dot(p.astype(BF16), v, preferred_element_type=F32) / l
    return o, m + jnp.log(l)


def _attn_kernel(q_ref, k_ref, v_ref, bias_ref, o_ref, qf, kf, vf, og, lg):
    S = q_ref.shape[0]
    QB = ATTN_Q_BLOCK
    qf[...] = q_ref[...].astype(F32)
    kf[...] = k_ref[...].astype(F32)
    vf[...] = v_ref[...].astype(F32)

    for g, (_, d) in enumerate(DILATED_BRANCHES):
        L = S // d
        assert L % QB == 0 and (L == QB or L >= 3 * QB)
        for r in range(d):
            for n in range(L // QB):
                q0 = n * QB
                if L == QB:
                    ks, W = 0, QB
                else:
                    ks, W = min(max(q0 - BAND_HALF, 0), L - 2 * QB), 2 * QB
                case = _ATTN_WINDOW_DELTAS.index(ks - q0)
                if d == 1:
                    qb = q_ref[q0:q0 + QB, :]
                    kb = k_ref[ks:ks + W, :]
                    vb = v_ref[ks:ks + W, :]
                else:
                    qb = qf[pl.ds(r + d * q0, QB, stride=d), :].astype(BF16)
                    kb = kf[pl.ds(r + d * ks, W, stride=d), :].astype(BF16)
                    vb = vf[pl.ds(r + d * ks, W, stride=d), :].astype(BF16)
                bias = bias_ref[g, case, 0][:, :W]
                o, lse = _band_block(qb, kb, vb, bias)
                lse = jnp.broadcast_to(lse, (QB, LANES))
                if d == 1:
                    og[g, q0:q0 + QB, :] = o
                    lg[g, q0:q0 + QB, :] = lse
                else:
                    og[g, pl.ds(r + d * q0, QB, stride=d), :] = o
                    lg[g, pl.ds(r + d * q0, QB, stride=d), :] = lse

    l0, l1, l2 = lg[0], lg[1], lg[2]
    mx = jnp.maximum(jnp.maximum(l0, l1), l2)
    e0, e1, e2 = jnp.exp(l0 - mx), jnp.exp(l1 - mx), jnp.exp(l2 - mx)
    out = (e0 * og[0] + e1 * og[1] + e2 * og[2]) / (e0 + e1 + e2)
    o_ref[...] = out.astype(o_ref.dtype)


def attention(proj, bias_tiles, *, batch, seq):
    H, Dh = ATTN_HEADS, ATTN_HEAD_DIM
    assert Dh == LANES
    T = batch * seq
    blk = lambda off: pl.BlockSpec((seq, Dh), lambda h, b: (b, off + h))
    return pl.pallas_call(
        _attn_kernel,
        grid=(H, batch),
        in_specs=[blk(0), blk(H), blk(2 * H),
                  pl.BlockSpec((3, 3, 1, ATTN_Q_BLOCK, 2 * ATTN_Q_BLOCK),
                               lambda h, b: (0, 0, h, 0, 0))],
        out_specs=pl.BlockSpec((seq, Dh), lambda h, b: (b, h)),
        out_shape=jax.ShapeDtypeStruct((T, H * Dh), BF16),
        scratch_shapes=[pltpu.VMEM((seq, Dh), F32)] * 3
                       + [pltpu.VMEM((3, seq, Dh), F32)] * 2,
        compiler_params=_params("parallel", "parallel"),
        name="dilated_attention",
    )(proj, proj, proj, bias_tiles)


def _log_sigmoid(x):
    return jnp.minimum(x, 0.0) - jnp.log1p(jnp.exp(-jnp.abs(x)))


def _mlstm_kernel(mq_ref, mk_ref, mv_ref, mo_ref, cwq_ref, cwk_ref, gc_ref, gr_ref,
                  gbr_ref, gbc_ref, o_ref, xp, qs, ks, hs, ct, nst, mst):
    S, Dh = mq_ref.shape
    C = MLSTM_CHUNK
    nc = S // C
    PAD = 8
    halo = CONV_WIDTH // 2

    def conv_silu(src_ref, w_ref, dst, scale):
        xp[0:PAD, :] = jnp.zeros((PAD, Dh), F32)
        xp[PAD + S:2 * PAD + S, :] = jnp.zeros((PAD, Dh), F32)
        xp[PAD:PAD + S, :] = src_ref[...].astype(F32)
        w = w_ref[...]
        for c in range(nc):
            acc = None
            for j in range(CONV_WIDTH):
                lo = PAD + c * C + j - halo
                term = xp[lo:lo + C, :] * w[j:j + 1, :]
                acc = term if acc is None else acc + term
            y = acc * jax.nn.sigmoid(acc)
            dst[c * C:(c + 1) * C, :] = (y * scale).astype(BF16)

    conv_silu(mq_ref, cwq_ref, qs, 1.0)
    conv_silu(mk_ref, cwk_ref, ks, Dh ** -0.5)

    ti = lax.broadcasted_iota(jnp.int32, (C, C), 0)
    si = lax.broadcasted_iota(jnp.int32, (C, C), 1)
    gb_row = gbr_ref[0]
    gb_col = gbc_ref[0]

    def chunk_step(c, forward):
        r0 = pl.multiple_of(c * C, C)
        q = qs[pl.ds(r0, C), :]
        k = ks[pl.ds(r0, C), :]
        v = mv_ref[pl.ds(r0, C), :]
        gcol = gc_ref[0, 0, pl.ds(r0, C), :] + gb_row
        grow = gr_ref[0, 0, c] + gb_col
        gi = 0 if forward else 2
        li_c = gcol[:, gi:gi + 1]
        lf_c = _log_sigmoid(gcol[:, gi + 1:gi + 2])
        li_r = grow[gi:gi + 1, :]
        lf_r = _log_sigmoid(grow[gi + 1:gi + 2, :])
        feeds = (si <= ti) if forward else (si >= ti)
        feeds_t = (ti <= si) if forward else (ti >= si)
        b_col = jnp.sum(jnp.where(feeds, lf_r, 0.0), axis=1, keepdims=True)
        b_row = jnp.sum(jnp.where(feeds_t, lf_c, 0.0), axis=0, keepdims=True)
        dm = jnp.where(feeds, b_col - b_row + li_r, NEG_INF)
        m_prev = mst[...]
        inter = b_col + m_prev
        mt = jnp.maximum(inter, jnp.max(dm, axis=1, keepdims=True))
        w_intra = jnp.exp(dm - mt)
        w_inter = jnp.exp(inter - mt)
        sc = lax.dot_general(q, k, (((1,), (1,)), ((), ())),
                             preferred_element_type=F32) * w_intra
        num = (jnp.dot(sc.astype(BF16), v, preferred_element_type=F32)
               + w_inter * jnp.dot(q, ct[...].astype(BF16), preferred_element_type=F32))
        qn = jnp.sum(q.astype(F32) * nst[...], axis=1, keepdims=True)
        den = jnp.sum(sc, axis=1, keepdims=True) + w_inter * qn
        h = num / jnp.maximum(jnp.abs(den), jnp.exp(-mt))
        bl = jnp.sum(lf_r, axis=1, keepdims=True)
        g_r = bl - b_row + li_r
        g_c = bl - b_col + li_c
        m_new = jnp.maximum(bl + m_prev, jnp.max(g_r, axis=1, keepdims=True))
        decay = jnp.exp(bl + m_prev - m_new)
        wk_c = jnp.exp(g_c - m_new)
        vw = (v.astype(F32) * wk_c).astype(BF16)
        ct[...] = decay * ct[...] + lax.dot_general(
            k, vw, (((0,), (0,)), ((), ())), preferred_element_type=F32)
        nst[...] = decay * nst[...] + jnp.sum(k.astype(F32) * wk_c, axis=0, keepdims=True)
        mst[...] = m_new
        return r0, h

    def reset_state():
        ct[...] = jnp.zeros_like(ct)
        nst[...] = jnp.zeros_like(nst)
        mst[...] = jnp.zeros_like(mst)

    reset_state()

    def fwd_body(i, carry):
        r0, h = chunk_step(i, True)
        hs[pl.ds(r0, C), :] = h
        return carry

    lax.fori_loop(0, nc, fwd_body, 0)
    reset_state()

    def bwd_body(i, carry):
        r0, h = chunk_step(nc - 1 - i, False)
        hs[pl.ds(r0, C), :] += h
        return carry

    lax.fori_loop(0, nc, bwd_body, 0)
    o_ref[...] = (jax.nn.sigmoid(mo_ref[...].astype(F32)) * hs[...]).astype(o_ref.dtype)


def mlstm(proj, gates, conv_w, gate_b, *, batch, seq, col0):
    H, Dh, C = MLSTM_HEADS, MLSTM_HEAD_DIM, MLSTM_CHUNK
    T = batch * seq
    nc = seq // C
    width = H * Dh
    c0 = col0 // Dh
    g = gates[:, :4 * H].reshape(batch, seq, 4, H)
    g_col = g.transpose(0, 3, 1, 2)
    g_row = g.transpose(0, 3, 2, 1).reshape(batch, H, 4, nc, C).transpose(0, 1, 3, 2, 4)
    gb = gate_b.reshape(4, H).T
    gb_row = gb.reshape(H, 1, 4)
    gb_col = gb.reshape(H, 4, 1)
    blk = lambda j: pl.BlockSpec((seq, Dh), lambda b, h: (b, c0 + j * H + h))
    return pl.pallas_call(
        _mlstm_kernel,
        grid=(batch, H),
        in_specs=[blk(0), blk(1), blk(2), blk(3),
                  pl.BlockSpec((CONV_WIDTH, Dh), lambda b, h: (0, h)),
                  pl.BlockSpec((CONV_WIDTH, Dh), lambda b, h: (0, H + h)),
                  pl.BlockSpec((1, 1, seq, 4), lambda b, h: (b, h, 0, 0)),
                  pl.BlockSpec((1, 1, nc, 4, C), lambda b, h: (b, h, 0, 0, 0)),
                  pl.BlockSpec((1, 1, 4), lambda b, h: (h, 0, 0)),
                  pl.BlockSpec((1, 4, 1), lambda b, h: (h, 0, 0))],
        out_specs=pl.BlockSpec((seq, Dh), lambda b, h: (b, h)),
        out_shape=jax.ShapeDtypeStruct((T, width), BF16),
        scratch_shapes=[pltpu.VMEM((seq + 16, Dh), F32),
                        pltpu.VMEM((seq, Dh), BF16),
                        pltpu.VMEM((seq, Dh), BF16),
                        pltpu.VMEM((seq, Dh), F32),
                        pltpu.VMEM((Dh, Dh), F32),
                        pltpu.VMEM((1, Dh), F32),
                        pltpu.VMEM((1, 1), F32)],
        compiler_params=_params("parallel", "parallel"),
        name="bidir_mlstm",
    )(proj, proj, proj, proj, conv_w, conv_w, g_col, g_row, gb_row, gb_col)


def _router_kernel(x_ref, wr_ref, br_ref, g_ref):
    logits = jnp.dot(x_ref[...], wr_ref[...], preferred_element_type=F32,
                     precision=lax.Precision.HIGHEST) + br_ref[...]
    lane = lax.broadcasted_iota(jnp.int32, logits.shape, 1)
    lg = jnp.where(lane < N_EXPERTS, logits, -jnp.inf)
    v1 = jnp.max(lg, axis=1, keepdims=True)
    i1 = jnp.min(jnp.where(lg == v1, lane, LANES), axis=1, keepdims=True)
    lg2 = jnp.where(lane == i1, -jnp.inf, lg)
    v2 = jnp.max(lg2, axis=1, keepdims=True)
    i2 = jnp.min(jnp.where(lg2 == v2, lane, LANES), axis=1, keepdims=True)
    e = jnp.exp(v2 - v1)
    p1 = 1.0 / (1.0 + e)
    p2 = e / (1.0 + e)
    g_ref[...] = jnp.where(lane == i1, p1, 0.0) + jnp.where(lane == i2, p2, 0.0)


def router_gates(x, wr, br, *, tm):
    T, D = x.shape
    wr_p = jnp.zeros((D, LANES), F32).at[:, :N_EXPERTS].set(wr)
    br_p = jnp.zeros((1, LANES), F32).at[0, :N_EXPERTS].set(br)
    return pl.pallas_call(
        _router_kernel,
        grid=(T // tm,),
        in_specs=[pl.BlockSpec((tm, D), lambda i: (i, 0)),
                  pl.BlockSpec((D, LANES), lambda i: (0, 0)),
                  pl.BlockSpec((1, LANES), lambda i: (0, 0))],
        out_specs=pl.BlockSpec((tm, LANES), lambda i: (i, 0)),
        out_shape=jax.ShapeDtypeStruct((T, LANES), F32),
        compiler_params=_params("parallel"),
        name="moe_router",
    )(x, wr_p, br_p)


def kernel(x, w_in, w_out, conv_w, gate_b, rpb_table, ln_g, ln_b, dense_w1, dense_w3,
           dense_w2, router_w, router_b, moe_w1, moe_w3, moe_w2):
    B, S, D = x.shape
    T = B * S
    depth = w_in.shape[0]
    alpha = (2 * depth) ** 0.25
    attn_w = ATTN_HEADS * ATTN_HEAD_DIM
    mlstm_w = MLSTM_HEADS * MLSTM_HEAD_DIM
    main_cols = 3 * attn_w + 4 * mlstm_w

    bias_tiles = _attn_bias_tiles(rpb_table)
    xf = x.reshape(T, D)
    xb = xf.astype(BF16)
    for l in range(depth):
        w_main = w_in[l, :, :main_cols].astype(BF16)
        w_gate = jnp.pad(w_in[l, :, main_cols:], ((0, 0), (0, LANES - 4 * MLSTM_HEADS))).astype(BF16)
        proj = matmul(xb, w_main, tm=1024, tn=1024, out_dtype=BF16, name="in_proj")
        gates = matmul(xb, w_gate, tm=2048, tn=LANES, out_dtype=F32, name="gate_proj")
        attn = attention(proj, bias_tiles, batch=B, seq=S)
        rec = mlstm(proj, gates, conv_w[l], gate_b[l], batch=B, seq=S, col0=3 * attn_w)
        mixed = jnp.concatenate([attn, rec], axis=1)
        xf, xb = matmul_epilogue(mixed, w_out[l].astype(BF16), tm=512, tk=1024,
                                 ln=(xf, ln_g[l, 0], ln_b[l, 0]), alpha=alpha, name="out_proj_ln")
        j = l // 2
        ln2 = (xf, ln_g[l, 1], ln_b[l, 1])
        if l % 2 == 0:
            hmid = swiglu_matmul(xb, dense_w1[j].astype(BF16), dense_w3[j].astype(BF16),
                                 tm=1024, tf=1408, name="dense_swiglu")
            xf, xb = matmul_epilogue(hmid, dense_w2[j].astype(BF16), tm=512, tk=1408,
                                     ln=ln2, alpha=alpha, name="dense_w2_ln")
        else:
            g = router_gates(xf, router_w[j], router_b[j], tm=1024)
            acc = None
            for e in range(N_EXPERTS):
                hmid = swiglu_matmul(xb, moe_w1[j, e].astype(BF16), moe_w3[j, e].astype(BF16),
                                     tm=1024, tf=1408, name="expert_swiglu")
                last = e == N_EXPERTS - 1
                acc = matmul_epilogue(hmid, moe_w2[j, e].astype(BF16), tm=512, tk=1408,
                                      gates=g, gate_col=e, prev=acc,
                                      ln=ln2 if last else None, alpha=alpha,
                                      name="expert_w2_ln" if last else "expert_w2")
            xf, xb = acc
    return xf.reshape(B, S, D)
```

```python
import functools
import math

import jax
import jax.numpy as jnp
from jax import lax
from jax.experimental import pallas as pl
from jax.experimental.pallas import tpu as pltpu

F32 = jnp.float32
BF16 = jnp.bfloat16

ATTN_HEADS = 8
ATTN_HEAD_DIM = 128
MLSTM_HEADS = 4
MLSTM_HEAD_DIM = 256
DILATED_BRANCHES = ((128, 1), (512, 4), (2048, 16))
BAND_HALF = 64
REL_BUCKETS = 32
REL_MAX_DIST = 1024
NEG_INF = -1e30
CONV_WIDTH = 5
N_EXPERTS = 8
LN_EPS = 1e-5

V7X_VMEM_BYTES = 64 * 1024 * 1024
VMEM_LIMIT_BYTES = V7X_VMEM_BYTES - 8 * 1024 * 1024
LANES = 128

ATTN_Q_BLOCK = 128
MLSTM_CHUNK = 256


def _params(*semantics):
    return pltpu.CompilerParams(dimension_semantics=semantics,
                                vmem_limit_bytes=VMEM_LIMIT_BYTES)


def _mm_kernel(a_ref, b_ref, o_ref):
    o_ref[...] = jnp.dot(a_ref[...], b_ref[...],
                         preferred_element_type=F32).astype(o_ref.dtype)


def matmul(a, b, *, tm, tn, out_dtype, name):
    M, K = a.shape
    N = b.shape[1]
    assert M % tm == 0 and N % tn == 0
    return pl.pallas_call(
        _mm_kernel,
        grid=(M // tm, N // tn),
        in_specs=[pl.BlockSpec((tm, K), lambda i, j: (i, 0)),
                  pl.BlockSpec((K, tn), lambda i, j: (0, j))],
        out_specs=pl.BlockSpec((tm, tn), lambda i, j: (i, j)),
        out_shape=jax.ShapeDtypeStruct((M, N), out_dtype),
        compiler_params=_params("parallel", "parallel"),
        name=name,
    )(a, b)


def _swiglu_kernel(a_ref, w1_ref, w3_ref, o_ref):
    a = a_ref[...]
    h1 = jnp.dot(a, w1_ref[...], preferred_element_type=F32)
    h3 = jnp.dot(a, w3_ref[...], preferred_element_type=F32)
    o_ref[...] = (h1 * jax.nn.sigmoid(h1) * h3).astype(o_ref.dtype)


def swiglu_matmul(a, w1, w3, *, tm, tf, name):
    M, K = a.shape
    F = w1.shape[1]
    assert M % tm == 0 and F % tf == 0
    return pl.pallas_call(
        _swiglu_kernel,
        grid=(M // tm, F // tf),
        in_specs=[pl.BlockSpec((tm, K), lambda i, j: (i, 0)),
                  pl.BlockSpec((K, tf), lambda i, j: (0, j)),
                  pl.BlockSpec((K, tf), lambda i, j: (0, j))],
        out_specs=pl.BlockSpec((tm, tf), lambda i, j: (i, j)),
        out_shape=jax.ShapeDtypeStruct((M, F), BF16),
        compiler_params=_params("parallel", "parallel"),
        name=name,
    )(a, w1, w3)


def _mm_epilogue_kernel(*refs, nk, gate_col, has_prev, has_ln, alpha):
    it = iter(refs)
    a_ref, b_ref = next(it), next(it)
    gate_ref = next(it) if gate_col is not None else None
    prev_ref = next(it) if has_prev else None
    if has_ln:
        res_ref, lng_ref, lnb_ref = next(it), next(it), next(it)
    o_ref = next(it)
    ob_ref = next(it) if has_ln else None
    acc_ref = next(it)

    k = pl.program_id(1)
    part = jnp.dot(a_ref[...], b_ref[...], preferred_element_type=F32)

    @pl.when(k == 0)
    def _():
        acc_ref[...] = part

    @pl.when(k > 0)
    def _():
        acc_ref[...] += part

    @pl.when(k == nk - 1)
    def _():
        y = acc_ref[...]
        if gate_ref is not None:
            y = y * gate_ref[:, gate_col:gate_col + 1]
        if prev_ref is not None:
            y = y + prev_ref[...]
        if has_ln:
            z = alpha * res_ref[...] + y
            mu = jnp.mean(z, axis=-1, keepdims=True)
            zc = z - mu
            var = jnp.mean(zc * zc, axis=-1, keepdims=True)
            out = zc * lax.rsqrt(var + LN_EPS) * lng_ref[...] + lnb_ref[...]
            o_ref[...] = out
            ob_ref[...] = out.astype(BF16)
        else:
            o_ref[...] = y


def matmul_epilogue(a, b, *, tm, tk, name, gates=None, gate_col=None, prev=None,
                    ln=None, alpha=None):
    M, K = a.shape
    N = b.shape[1]
    assert M % tm == 0 and K % tk == 0
    nk = K // tk
    row = lambda i, k: (i, 0)
    inputs = [a, b]
    in_specs = [pl.BlockSpec((tm, tk), lambda i, k: (i, k)),
                pl.BlockSpec((tk, N), lambda i, k: (k, 0))]
    if gates is not None:
        inputs.append(gates)
        in_specs.append(pl.BlockSpec((tm, gates.shape[1]), row))
    if prev is not None:
        inputs.append(prev)
        in_specs.append(pl.BlockSpec((tm, N), row))
    if ln is not None:
        res, g, beta = ln
        inputs += [res, g.reshape(1, N), beta.reshape(1, N)]
        in_specs += [pl.BlockSpec((tm, N), row),
                     pl.BlockSpec((1, N), lambda i, k: (0, 0)),
                     pl.BlockSpec((1, N), lambda i, k: (0, 0))]
    out_shape = [jax.ShapeDtypeStruct((M, N), F32)]
    out_specs = [pl.BlockSpec((tm, N), row)]
    if ln is not None:
        out_shape.append(jax.ShapeDtypeStruct((M, N), BF16))
        out_specs.append(pl.BlockSpec((tm, N), row))
    kern = functools.partial(_mm_epilogue_kernel, nk=nk,
                             gate_col=gate_col if gates is not None else None,
                             has_prev=prev is not None, has_ln=ln is not None, alpha=alpha)
    outs = pl.pallas_call(
        kern,
        grid=(M // tm, nk),
        in_specs=in_specs,
        out_specs=out_specs,
        out_shape=out_shape,
        scratch_shapes=[pltpu.VMEM((tm, N), F32)],
        compiler_params=_params("parallel", "arbitrary"),
        name=name,
    )(*inputs)
    return outs if ln is not None else outs[0]


def _t5_bucket(rel):
    half = REL_BUCKETS // 2
    max_exact = half // 2
    n = jnp.abs(rel)
    large = max_exact + (jnp.log(jnp.maximum(n, 1).astype(F32) / max_exact)
                         / math.log(REL_MAX_DIST / max_exact)
                         * (half - max_exact)).astype(jnp.int32)
    large = jnp.minimum(large, half - 1)
    return jnp.where(rel > 0, half, 0) + jnp.where(n < max_exact, n, large)


_ATTN_WINDOW_DELTAS = (0, -BAND_HALF, -2 * BAND_HALF)


def _attn_bias_tiles(table):
    qi = jnp.arange(ATTN_Q_BLOCK)[:, None]
    kj = jnp.arange(2 * ATTN_Q_BLOCK)[None, :]
    tiles = []
    for window, dilation in DILATED_BRANCHES:
        assert window // (2 * dilation) == BAND_HALF
        per_case = []
        for delta in _ATTN_WINDOW_DELTAS:
            m = kj + delta - qi
            onehot = jax.nn.one_hot(_t5_bucket(m * dilation), REL_BUCKETS, dtype=F32)
            bias = jnp.einsum("qkb,bh->qkh", onehot, table.astype(F32),
                              precision=lax.Precision.HIGHEST)
            bias = jnp.where((jnp.abs(m) <= BAND_HALF)[..., None], bias, NEG_INF)
            per_case.append(bias.transpose(2, 0, 1))
        tiles.append(jnp.stack(per_case))
    return jnp.stack(tiles)


def _band_block(q, k, v, bias):
    s = lax.dot_general(q, k, (((1,), (1,)), ((), ())), preferred_element_type=F32)
    s = s * (ATTN_HEAD_DIM ** -0.5) + bias
    m = jnp.max(s, axis=-1, keepdims=True)
    p = jnp.exp(s - m)
    l = jnp.sum(p, axis=-1, keepdims=True)
    o = jnp.dot(p.astype(BF16), v, preferred_element_type=F32) / l
    return o, m + jnp.log(l)


def _attn_kernel(q_ref, k_ref, v_ref, bias_ref, o_ref, qf, kf, vf, og, lg):
    S = q_ref.shape[0]
    QB = ATTN_Q_BLOCK
    qf[...] = q_ref[...].astype(F32)
    kf[...] = k_ref[...].astype(F32)
    vf[...] = v_ref[...].astype(F32)

    for g, (_, d) in enumerate(DILATED_BRANCHES):
        L = S // d
        assert L % QB == 0 and (L == QB or L >= 3 * QB)
        for r in range(d):
            for n in range(L // QB):
                q0 = n * QB
                if L == QB:
                    ks, W = 0, QB
                else:
                    ks, W = min(max(q0 - BAND_HALF, 0), L - 2 * QB), 2 * QB
                case = _ATTN_WINDOW_DELTAS.index(ks - q0)
                if d == 1:
                    qb = q_ref[q0:q0 + QB, :]
                    kb = k_ref[ks:ks + W, :]
                    vb = v_ref[ks:ks + W, :]
                else:
                    qb = qf[pl.ds(r + d * q0, QB, stride=d), :].astype(BF16)
                    kb = kf[pl.ds(r + d * ks, W, stride=d), :].astype(BF16)
                    vb = vf[pl.ds(r + d * ks, W, stride=d), :].astype(BF16)
                bias = bias_ref[g, case, 0][:, :W]
                o, lse = _band_block(qb, kb, vb, bias)
                lse = jnp.broadcast_to(lse, (QB, LANES))
                if d == 1:
                    og[g, q0:q0 + QB, :] = o
                    lg[g, q0:q0 + QB, :] = lse
                else:
                    og[g, pl.ds(r + d * q0, QB, stride=d), :] = o
                    lg[g, pl.ds(r + d * q0, QB, stride=d), :] = lse

    l0, l1, l2 = lg[0], lg[1], lg[2]
    mx = jnp.maximum(jnp.maximum(l0, l1), l2)
    e0, e1, e2 = jnp.exp(l0 - mx), jnp.exp(l1 - mx), jnp.exp(l2 - mx)
    out = (e0 * og[0] + e1 * og[1] + e2 * og[2]) / (e0 + e1 + e2)
    o_ref[...] = out.astype(o_ref.dtype)


def attention(proj, bias_tiles, *, batch, seq):
    H, Dh = ATTN_HEADS, ATTN_HEAD_DIM
    assert Dh == LANES
    T = batch * seq
    blk = lambda off: pl.BlockSpec((seq, Dh), lambda h, b: (b, off + h))
    return pl.pallas_call(
        _attn_kernel,
        grid=(H, batch),
        in_specs=[blk(0), blk(H), blk(2 * H),
                  pl.BlockSpec((3, 3, 1, ATTN_Q_BLOCK, 2 * ATTN_Q_BLOCK),
                               lambda h, b: (0, 0, h, 0, 0))],
        out_specs=pl.BlockSpec((seq, Dh), lambda h, b: (b, h)),
        out_shape=jax.ShapeDtypeStruct((T, H * Dh), BF16),
        scratch_shapes=[pltpu.VMEM((seq, Dh), F32)] * 3
                       + [pltpu.VMEM((3, seq, Dh), F32)] * 2,
        compiler_params=_params("parallel", "parallel"),
        name="dilated_attention",
    )(proj, proj, proj, bias_tiles)


def _log_sigmoid(x):
    return jnp.minimum(x, 0.0) - jnp.log1p(jnp.exp(-jnp.abs(x)))


def _mlstm_kernel(mq_ref, mk_ref, mv_ref, mo_ref, cwq_ref, cwk_ref, gc_ref, gr_ref,
                  gbr_ref, gbc_ref, o_ref, xp, qs, ks, hs, ct, nst, mst):
    S, Dh = mq_ref.shape
    C = MLSTM_CHUNK
    nc = S // C
    PAD = 8
    halo = CONV_WIDTH // 2

    def conv_silu(src_ref, w_ref, dst, scale):
        xp[0:PAD, :] = jnp.zeros((PAD, Dh), F32)
        xp[PAD + S:2 * PAD + S, :] = jnp.zeros((PAD, Dh), F32)
        xp[PAD:PAD + S, :] = src_ref[...].astype(F32)
        w = w_ref[...]
        for c in range(nc):
            acc = None
            for j in range(CONV_WIDTH):
                lo = PAD + c * C + j - halo
                term = xp[lo:lo + C, :] * w[j:j + 1, :]
                acc = term if acc is None else acc + term
            y = acc * jax.nn.sigmoid(acc)
            dst[c * C:(c + 1) * C, :] = (y * scale).astype(BF16)

    conv_silu(mq_ref, cwq_ref, qs, 1.0)
    conv_silu(mk_ref, cwk_ref, ks, Dh ** -0.5)

    ti = lax.broadcasted_iota(jnp.int32, (C, C), 0)
    si = lax.broadcasted_iota(jnp.int32, (C, C), 1)
    gb_row = gbr_ref[0]
    gb_col = gbc_ref[0]

    def chunk_step(c, forward):
        r0 = pl.multiple_of(c * C, C)
        q = qs[pl.ds(r0, C), :]
        k = ks[pl.ds(r0, C), :]
        v = mv_ref[pl.ds(r0, C), :]
        gcol = gc_ref[0, 0, pl.ds(r0, C), :] + gb_row
        grow = gr_ref[0, 0, c] + gb_col
        gi = 0 if forward else 2
        li_c = gcol[:, gi:gi + 1]
        lf_c = _log_sigmoid(gcol[:, gi + 1:gi + 2])
        li_r = grow[gi:gi + 1, :]
        lf_r = _log_sigmoid(grow[gi + 1:gi + 2, :])
        feeds = (si <= ti) if forward else (si >= ti)
        feeds_t = (ti <= si) if forward else (ti >= si)
        b_col = jnp.sum(jnp.where(feeds, lf_r, 0.0), axis=1, keepdims=True)
        b_row = jnp.sum(jnp.where(feeds_t, lf_c, 0.0), axis=0, keepdims=True)
        dm = jnp.where(feeds, b_col - b_row + li_r, NEG_INF)
        m_prev = mst[...]
        inter = b_col + m_prev
        mt = jnp.maximum(inter, jnp.max(dm, axis=1, keepdims=True))
        w_intra = jnp.exp(dm - mt)
        w_inter = jnp.exp(inter - mt)
        sc = lax.dot_general(q, k, (((1,), (1,)), ((), ())),
                             preferred_element_type=F32) * w_intra
        num = (jnp.dot(sc.astype(BF16), v, preferred_element_type=F32)
               + w_inter * jnp.dot(q, ct[...].astype(BF16), preferred_element_type=F32))
        qn = jnp.sum(q.astype(F32) * nst[...], axis=1, keepdims=True)
        den = jnp.sum(sc, axis=1, keepdims=True) + w_inter * qn
        h = num / jnp.maximum(jnp.abs(den), jnp.exp(-mt))
        bl = jnp.sum(lf_r, axis=1, keepdims=True)
        g_r = bl - b_row + li_r
        g_c = bl - b_col + li_c
        m_new = jnp.maximum(bl + m_prev, jnp.max(g_r, axis=1, keepdims=True))
        decay = jnp.exp(bl + m_prev - m_new)
        wk_c = jnp.exp(g_c - m_new)
        vw = (v.astype(F32) * wk_c).astype(BF16)
        ct[...] = decay * ct[...] + lax.dot_general(
            k, vw, (((0,), (0,)), ((), ())), preferred_element_type=F32)
        nst[...] = decay * nst[...] + jnp.sum(k.astype(F32) * wk_c, axis=0, keepdims=True)
        mst[...] = m_new
        return r0, h

    def reset_state():
        ct[...] = jnp.zeros_like(ct)
        nst[...] = jnp.zeros_like(nst)
        mst[...] = jnp.zeros_like(mst)

    reset_state()

    def fwd_body(i, carry):
        r0, h = chunk_step(i, True)
        hs[pl.ds(r0, C), :] = h
        return carry

    lax.fori_loop(0, nc, fwd_body, 0)
    reset_state()

    def bwd_body(i, carry):
        r0, h = chunk_step(nc - 1 - i, False)
        hs[pl.ds(r0, C), :] += h
        return carry

    lax.fori_loop(0, nc, bwd_body, 0)
    o_ref[...] = (jax.nn.sigmoid(mo_ref[...].astype(F32)) * hs[...]).astype(o_ref.dtype)


def mlstm(proj, gates, conv_w, gate_b, *, batch, seq, col0):
    H, Dh, C = MLSTM_HEADS, MLSTM_HEAD_DIM, MLSTM_CHUNK
    T = batch * seq
    nc = seq // C
    width = H * Dh
    c0 = col0 // Dh
    g = gates[:, :4 * H].reshape(batch, seq, 4, H)
    g_col = g.transpose(0, 3, 1, 2)
    g_row = g.transpose(0, 3, 2, 1).reshape(batch, H, 4, nc, C).transpose(0, 1, 3, 2, 4)
    gb = gate_b.reshape(4, H).T
    gb_row = gb.reshape(H, 1, 4)
    gb_col = gb.reshape(H, 4, 1)
    blk = lambda j: pl.BlockSpec((seq, Dh), lambda b, h: (b, c0 + j * H + h))
    return pl.pallas_call(
        _mlstm_kernel,
        grid=(batch, H),
        in_specs=[blk(0), blk(1), blk(2), blk(3),
                  pl.BlockSpec((CONV_WIDTH, Dh), lambda b, h: (0, h)),
                  pl.BlockSpec((CONV_WIDTH, Dh), lambda b, h: (0, H + h)),
                  pl.BlockSpec((1, 1, seq, 4), lambda b, h: (b, h, 0, 0)),
                  pl.BlockSpec((1, 1, nc, 4, C), lambda b, h: (b, h, 0, 0, 0)),
                  pl.BlockSpec((1, 1, 4), lambda b, h: (h, 0, 0)),
                  pl.BlockSpec((1, 4, 1), lambda b, h: (h, 0, 0))],
        out_specs=pl.BlockSpec((seq, Dh), lambda b, h: (b, h)),
        out_shape=jax.ShapeDtypeStruct((T, width), BF16),
        scratch_shapes=[pltpu.VMEM((seq + 16, Dh), F32),
                        pltpu.VMEM((seq, Dh), BF16),
                        pltpu.VMEM((seq, Dh), BF16),
                        pltpu.VMEM((seq, Dh), F32),
                        pltpu.VMEM((Dh, Dh), F32),
                        pltpu.VMEM((1, Dh), F32),
                        pltpu.VMEM((1, 1), F32)],
        compiler_params=_params("parallel", "parallel"),
        name="bidir_mlstm",
    )(proj, proj, proj, proj, conv_w, conv_w, g_col, g_row, gb_row, gb_col)


def _router_kernel(x_ref, wr_ref, br_ref, g_ref, rank_ref, cnt_ref):
    tm = x_ref.shape[0]

    @pl.when(pl.program_id(0) == 0)
    def _():
        cnt_ref[...] = jnp.zeros_like(cnt_ref)

    logits = jnp.dot(x_ref[...], wr_ref[...], preferred_element_type=F32,
                     precision=lax.Precision.HIGHEST) + br_ref[...]
    lane = lax.broadcasted_iota(jnp.int32, logits.shape, 1)
    lg = jnp.where(lane < N_EXPERTS, logits, -jnp.inf)
    v1 = jnp.max(lg, axis=1, keepdims=True)
    i1 = jnp.min(jnp.where(lg == v1, lane, LANES), axis=1, keepdims=True)
    lg2 = jnp.where(lane == i1, -jnp.inf, lg)
    v2 = jnp.max(lg2, axis=1, keepdims=True)
    i2 = jnp.min(jnp.where(lg2 == v2, lane, LANES), axis=1, keepdims=True)
    e = jnp.exp(v2 - v1)
    p1 = 1.0 / (1.0 + e)
    p2 = e / (1.0 + e)
    g_ref[...] = jnp.where(lane == i1, p1, 0.0) + jnp.where(lane == i2, p2, 0.0)
    sel = (lane == i1) | (lane == i2)
    self_ = jnp.where(sel, 1.0, 0.0)
    row = lax.broadcasted_iota(jnp.int32, (tm, tm), 0)
    col = lax.broadcasted_iota(jnp.int32, (tm, tm), 1)
    earlier = jnp.where(col < row, 1.0, 0.0).astype(BF16)
    raw = (jnp.dot(earlier, self_.astype(BF16), preferred_element_type=F32)
           + cnt_ref[...]).astype(jnp.int32)
    rank_ref[...] = jnp.where(sel, raw, -1 - raw)
    cnt_ref[...] += jnp.sum(self_, axis=0, keepdims=True)


def router(x, wr, br, *, tm):
    T, D = x.shape
    wr_p = jnp.zeros((D, LANES), F32).at[:, :N_EXPERTS].set(wr)
    br_p = jnp.zeros((1, LANES), F32).at[0, :N_EXPERTS].set(br)
    return pl.pallas_call(
        _router_kernel,
        grid=(T // tm,),
        in_specs=[pl.BlockSpec((tm, D), lambda i: (i, 0)),
                  pl.BlockSpec((D, LANES), lambda i: (0, 0)),
                  pl.BlockSpec((1, LANES), lambda i: (0, 0))],
        out_specs=[pl.BlockSpec((tm, LANES), lambda i: (i, 0)),
                   pl.BlockSpec((tm, LANES), lambda i: (i, 0))],
        out_shape=[jax.ShapeDtypeStruct((T, LANES), F32),
                   jax.ShapeDtypeStruct((T, LANES), jnp.int32)],
        scratch_shapes=[pltpu.VMEM((1, LANES), F32)],
        compiler_params=_params("arbitrary"),
        name="moe_router",
    )(x, wr_p, br_p)


MOE_TILE = 256
MOE_TOKEN_BLOCK = 512


def _moe_plan(rank, *, tile, tb):
    T = rank.shape[0]
    E = N_EXPERTS
    nb = T // tb
    n_tiles = 2 * T // tile + E
    rows_max = n_tiles * tile
    r = rank[:, :E]
    sel = r >= 0
    raw = jnp.where(sel, r, -1 - r)
    cb = raw[::tb]
    counts = raw[-1] + sel[-1].astype(jnp.int32)
    padded = (counts + tile - 1) // tile * tile
    ends = jnp.cumsum(padded)
    start = ends - padded
    pos = jnp.where(sel, start[None, :] + r, -1)
    pos_pad = jnp.full((T, LANES), -1, jnp.int32).at[:, :E].set(pos)
    lo = (start[None, :] + cb).T.reshape(-1)
    hi = (start[None, :] + jnp.concatenate([cb[1:], counts[None, :]], axis=0)).T.reshape(-1)
    tile_starts = jnp.arange(n_tiles, dtype=jnp.int32) * tile
    bounds = jnp.sort(jnp.concatenate([tile_starts, lo]))
    nxt = jnp.concatenate([bounds[1:], jnp.array([rows_max], jnp.int32)])
    pair = jnp.clip(jnp.searchsorted(lo, bounds, side="right") - 1, 0, E * nb - 1)
    seg_hi = jnp.minimum(nxt, hi[pair])
    seg_valid = (seg_hi > bounds).astype(jnp.int32)
    seg_tile = jnp.minimum(bounds // tile, n_tiles - 1).astype(jnp.int32)
    seg_e = (pair // nb).astype(jnp.int32)
    seg_b = (pair % nb).astype(jnp.int32)
    changed = lambda a: jnp.concatenate([jnp.ones((1,), jnp.int32),
                                         (a[1:] != a[:-1]).astype(jnp.int32)])
    gather_plan = (seg_tile, seg_b, seg_e, seg_valid, changed(seg_tile))
    n_seg = bounds.shape[0]
    order = jnp.argsort(seg_b * n_seg + jnp.arange(n_seg, dtype=jnp.int32))
    cb_, ct_, ce_, cv_ = seg_b[order], seg_tile[order], seg_e[order], seg_valid[order]
    first = changed(cb_)
    last = jnp.concatenate([first[1:], jnp.ones((1,), jnp.int32)])
    combine_plan = (ct_, cb_, ce_, cv_, first, last)
    tile_expert = jnp.clip(jnp.searchsorted(ends, tile_starts, side="right"), 0, E - 1).astype(jnp.int32)
    tile_valid = (tile_starts < ends[-1]).astype(jnp.int32)
    return pos_pad, gather_plan, combine_plan, tile_expert, tile_valid, n_tiles


def _gather_kernel(seg_tile, seg_blk, seg_e, seg_valid, seg_first, x_ref, post_ref, o_ref):
    s = pl.program_id(0)
    tile, tb = o_ref.shape[0], x_ref.shape[0]

    @pl.when(seg_first[s] == 1)
    def _():
        o_ref[...] = jnp.zeros_like(o_ref)

    @pl.when(seg_valid[s] == 1)
    def _():
        rel = post_ref[pl.ds(seg_e[s], 1), :] - seg_tile[s] * tile
        rid = lax.broadcasted_iota(jnp.int32, (tile, tb), 0)
        onehot = jnp.where(rel == rid, 1.0, 0.0).astype(BF16)
        o_ref[...] += jnp.dot(onehot, x_ref[...], preferred_element_type=F32).astype(BF16)


def moe_gather(xb, pos_t, plan, *, n_tiles, tile, tb):
    T, D = xb.shape
    n_seg = plan[0].shape[0]
    grid_spec = pltpu.PrefetchScalarGridSpec(
        num_scalar_prefetch=5, grid=(n_seg,),
        in_specs=[pl.BlockSpec((tb, D), lambda s, st, sb, se, sv, sf: (sb[s], 0)),
                  pl.BlockSpec((N_EXPERTS, tb), lambda s, st, sb, se, sv, sf: (0, sb[s]))],
        out_specs=pl.BlockSpec((tile, D), lambda s, st, sb, se, sv, sf: (st[s], 0)))
    return pl.pallas_call(
        _gather_kernel, grid_spec=grid_spec,
        out_shape=jax.ShapeDtypeStruct((n_tiles * tile, D), BF16),
        compiler_params=_params("arbitrary"), name="moe_gather",
    )(*plan, xb, pos_t)


def _expert_swiglu_kernel(te, tv, x_ref, w1_ref, w3_ref, o_ref):
    i = pl.program_id(1)

    @pl.when(tv[i] == 1)
    def _():
        a = x_ref[...]
        h1 = jnp.dot(a, w1_ref[0], preferred_element_type=F32)
        h3 = jnp.dot(a, w3_ref[0], preferred_element_type=F32)
        o_ref[...] = (h1 * jax.nn.sigmoid(h1) * h3).astype(o_ref.dtype)

    @pl.when(tv[i] == 0)
    def _():
        o_ref[...] = jnp.zeros_like(o_ref)


def expert_swiglu(xs, w1, w3, tile_expert, tile_valid, *, tile, tf):
    R, D = xs.shape
    E, _, F = w1.shape
    n_tiles = R // tile
    grid_spec = pltpu.PrefetchScalarGridSpec(
        num_scalar_prefetch=2, grid=(F // tf, n_tiles),
        in_specs=[pl.BlockSpec((tile, D), lambda f, i, te, tv: (i, 0)),
                  pl.BlockSpec((1, D, tf), lambda f, i, te, tv: (te[i], 0, f)),
                  pl.BlockSpec((1, D, tf), lambda f, i, te, tv: (te[i], 0, f))],
        out_specs=pl.BlockSpec((tile, tf), lambda f, i, te, tv: (i, f)))
    return pl.pallas_call(
        _expert_swiglu_kernel, grid_spec=grid_spec,
        out_shape=jax.ShapeDtypeStruct((R, F), BF16),
        compiler_params=_params("parallel", "parallel"), name="expert_swiglu",
    )(tile_expert, tile_valid, xs, w1, w3)


def _expert_w2_kernel(te, tv, h_ref, w2_ref, o_ref):
    i = pl.program_id(0)

    @pl.when(tv[i] == 1)
    def _():
        o_ref[...] = jnp.dot(h_ref[...], w2_ref[0], preferred_element_type=F32).astype(o_ref.dtype)

    @pl.when(tv[i] == 0)
    def _():
        o_ref[...] = jnp.zeros_like(o_ref)


def expert_w2(h, w2, tile_expert, tile_valid, *, tile):
    R, F = h.shape
    D = w2.shape[2]
    grid_spec = pltpu.PrefetchScalarGridSpec(
        num_scalar_prefetch=2, grid=(R // tile,),
        in_specs=[pl.BlockSpec((tile, F), lambda i, te, tv: (i, 0)),
                  pl.BlockSpec((1, F, D), lambda i, te, tv: (te[i], 0, 0))],
        out_specs=pl.BlockSpec((tile, D), lambda i, te, tv: (i, 0)))
    return pl.pallas_call(
        _expert_w2_kernel, grid_spec=grid_spec,
        out_shape=jax.ShapeDtypeStruct((R, D), BF16),
        compiler_params=_params("parallel"), name="expert_w2",
    )(tile_expert, tile_valid, h, w2)


def _combine_kernel(seg_tile, seg_blk, seg_e, seg_valid, seg_first, seg_last,
                    y_ref, pos_ref, g_ref, res_ref, lng_ref, lnb_ref, o_ref, ob_ref, acc_ref,
                    *, alpha):
    s = pl.program_id(0)
    tile, tb = y_ref.shape[0], pos_ref.shape[0]

    @pl.when(seg_first[s] == 1)
    def _():
        acc_ref[...] = jnp.zeros_like(acc_ref)

    @pl.when(seg_valid[s] == 1)
    def _():
        lane = lax.broadcasted_iota(jnp.int32, (tb, LANES), 1)
        mine = lane == seg_e[s]
        pcol = jnp.sum(jnp.where(mine, pos_ref[...], 0.0), axis=1, keepdims=True)
        gcol = jnp.sum(jnp.where(mine, g_ref[...], 0.0), axis=1, keepdims=True)
        rel = pcol - (seg_tile[s] * tile).astype(F32)
        cid = lax.broadcasted_iota(jnp.int32, (tb, tile), 1).astype(F32)
        weights = jnp.where(rel == cid, gcol, 0.0).astype(BF16)
        acc_ref[...] += jnp.dot(weights, y_ref[...], preferred_element_type=F32)

    @pl.when(seg_last[s] == 1)
    def _():
        z = alpha * res_ref[...] + acc_ref[...]
        mu = jnp.mean(z, axis=-1, keepdims=True)
        zc = z - mu
        var = jnp.mean(zc * zc, axis=-1, keepdims=True)
        out = zc * lax.rsqrt(var + LN_EPS) * lng_ref[...] + lnb_ref[...]
        o_ref[...] = out
        ob_ref[...] = out.astype(BF16)


def moe_combine_ln(y, pos_f, gates, res, g, beta, plan, *, tile, tb, alpha):
    T, D = res.shape
    n_seg = plan[0].shape[0]
    tok = lambda s, st, sb, se, sv, sf, sl: (sb[s], 0)
    const = lambda s, st, sb, se, sv, sf, sl: (0, 0)
    grid_spec = pltpu.PrefetchScalarGridSpec(
        num_scalar_prefetch=6, grid=(n_seg,),
        in_specs=[pl.BlockSpec((tile, D), lambda s, st, sb, se, sv, sf, sl: (st[s], 0)),
                  pl.BlockSpec((tb, LANES), tok), pl.BlockSpec((tb, LANES), tok),
                  pl.BlockSpec((tb, D), tok),
                  pl.BlockSpec((1, D), const), pl.BlockSpec((1, D), const)],
        out_specs=[pl.BlockSpec((tb, D), tok), pl.BlockSpec((tb, D), tok)],
        scratch_shapes=[pltpu.VMEM((tb, D), F32)])
    return pl.pallas_call(
        functools.partial(_combine_kernel, alpha=alpha), grid_spec=grid_spec,
        out_shape=[jax.ShapeDtypeStruct((T, D), F32), jax.ShapeDtypeStruct((T, D), BF16)],
        compiler_params=_params("arbitrary"), name="moe_combine_ln",
    )(*plan, y, pos_f, gates, res, g.reshape(1, D), beta.reshape(1, D))


def moe_layer(xf, xb, wr, br, w1, w3, w2, ln_g, ln_b, *, alpha):
    tile, tb = MOE_TILE, MOE_TOKEN_BLOCK
    gates, rank = router(xf, wr, br, tm=1024)
    pos, gather_plan, combine_plan, tile_expert, tile_valid, n_tiles = _moe_plan(rank, tile=tile, tb=tb)
    xs = moe_gather(xb, pos[:, :N_EXPERTS].T, gather_plan, n_tiles=n_tiles, tile=tile, tb=tb)
    h = expert_swiglu(xs, w1, w3, tile_expert, tile_valid, tile=tile, tf=1408)
    y = expert_w2(h, w2, tile_expert, tile_valid, tile=tile)
    return moe_combine_ln(y, pos.astype(F32), gates, xf, ln_g, ln_b, combine_plan,
                          tile=tile, tb=tb, alpha=alpha)


def kernel(x, w_in, w_out, conv_w, gate_b, rpb_table, ln_g, ln_b, dense_w1, dense_w3,
           dense_w2, router_w, router_b, moe_w1, moe_w3, moe_w2):
    B, S, D = x.shape
    T = B * S
    depth = w_in.shape[0]
    alpha = (2 * depth) ** 0.25
    attn_w = ATTN_HEADS * ATTN_HEAD_DIM
    mlstm_w = MLSTM_HEADS * MLSTM_HEAD_DIM
    main_cols = 3 * attn_w + 4 * mlstm_w

    bias_tiles = _attn_bias_tiles(rpb_table)
    xf = x.reshape(T, D)
    xb = xf.astype(BF16)
    for l in range(depth):
        w_main = w_in[l, :, :main_cols].astype(BF16)
        w_gate = jnp.pad(w_in[l, :, main_cols:], ((0, 0), (0, LANES - 4 * MLSTM_HEADS))).astype(BF16)
        proj = matmul(xb, w_main, tm=1024, tn=1024, out_dtype=BF16, name="in_proj")
        gates = matmul(xb, w_gate, tm=2048, tn=LANES, out_dtype=F32, name="gate_proj")
        attn = attention(proj, bias_tiles, batch=B, seq=S)
        rec = mlstm(proj, gates, conv_w[l], gate_b[l], batch=B, seq=S, col0=3 * attn_w)
        mixed = jnp.concatenate([attn, rec], axis=1)
        xf, xb = matmul_epilogue(mixed, w_out[l].astype(BF16), tm=512, tk=1024,
                                 ln=(xf, ln_g[l, 0], ln_b[l, 0]), alpha=alpha, name="out_proj_ln")
        j = l // 2
        ln2 = (xf, ln_g[l, 1], ln_b[l, 1])
        if l % 2 == 0:
            hmid = swiglu_matmul(xb, dense_w1[j].astype(BF16), dense_w3[j].astype(BF16),
                                 tm=1024, tf=1408, name="dense_swiglu")
            xf, xb = matmul_epilogue(hmid, dense_w2[j].astype(BF16), tm=512, tk=1408,
                                     ln=ln2, alpha=alpha, name="dense_w2_ln")
        else:
            xf, xb = moe_layer(xf, xb, router_w[j], router_b[j], moe_w1[j].astype(BF16),
                               moe_w3[j].astype(BF16), moe_w2[j].astype(BF16),
                               ln_g[l, 1], ln_b[l, 1], alpha=alpha)
    return xf.reshape(B, S, D)
```

```python
import functools
import math

import jax
import jax.numpy as jnp
from jax import lax
from jax.experimental import pallas as pl
from jax.experimental.pallas import tpu as pltpu

F32 = jnp.float32
BF16 = jnp.bfloat16

ATTN_HEADS = 8
ATTN_HEAD_DIM = 128
MLSTM_HEADS = 4
MLSTM_HEAD_DIM = 256
DILATED_BRANCHES = ((128, 1), (512, 4), (2048, 16))
BAND_HALF = 64
REL_BUCKETS = 32
REL_MAX_DIST = 1024
NEG_INF = -1e30
CONV_WIDTH = 5
N_EXPERTS = 8
LN_EPS = 1e-5

V7X_VMEM_BYTES = 64 * 1024 * 1024
VMEM_LIMIT_BYTES = V7X_VMEM_BYTES - 8 * 1024 * 1024
LANES = 128

ATTN_Q_BLOCK = 128
MLSTM_CHUNK = 256


def _params(*semantics):
    return pltpu.CompilerParams(dimension_semantics=semantics,
                                vmem_limit_bytes=VMEM_LIMIT_BYTES)


def _mm_kernel(a_ref, b_ref, o_ref):
    o_ref[...] = jnp.dot(a_ref[...], b_ref[...],
                         preferred_element_type=F32).astype(o_ref.dtype)


def matmul(a, b, *, tm, tn, out_dtype, name):
    M, K = a.shape
    N = b.shape[1]
    assert M % tm == 0 and N % tn == 0
    return pl.pallas_call(
        _mm_kernel,
        grid=(M // tm, N // tn),
        in_specs=[pl.BlockSpec((tm, K), lambda i, j: (i, 0)),
                  pl.BlockSpec((K, tn), lambda i, j: (0, j))],
        out_specs=pl.BlockSpec((tm, tn), lambda i, j: (i, j)),
        out_shape=jax.ShapeDtypeStruct((M, N), out_dtype),
        compiler_params=_params("parallel", "parallel"),
        name=name,
    )(a, b)


def _swiglu_kernel(a_ref, w1_ref, w3_ref, o_ref):
    a = a_ref[...]
    h1 = jnp.dot(a, w1_ref[...], preferred_element_type=F32)
    h3 = jnp.dot(a, w3_ref[...], preferred_element_type=F32)
    o_ref[...] = (h1 * jax.nn.sigmoid(h1) * h3).astype(o_ref.dtype)


def swiglu_matmul(a, w1, w3, *, tm, tf, name):
    M, K = a.shape
    F = w1.shape[1]
    assert M % tm == 0 and F % tf == 0
    return pl.pallas_call(
        _swiglu_kernel,
        grid=(M // tm, F // tf),
        in_specs=[pl.BlockSpec((tm, K), lambda i, j: (i, 0)),
                  pl.BlockSpec((K, tf), lambda i, j: (0, j)),
                  pl.BlockSpec((K, tf), lambda i, j: (0, j))],
        out_specs=pl.BlockSpec((tm, tf), lambda i, j: (i, j)),
        out_shape=jax.ShapeDtypeStruct((M, F), BF16),
        compiler_params=_params("parallel", "parallel"),
        name=name,
    )(a, w1, w3)


def _layer_norm_rows(z, g, b):
    mu = jnp.mean(z, axis=-1, keepdims=True)
    zc = z - mu
    var = jnp.mean(zc * zc, axis=-1, keepdims=True)
    return zc * lax.rsqrt(var + LN_EPS) * g + b


def _mm_res_ln_kernel(*refs, n_in, sub, alpha):
    a_refs, b_refs = refs[:n_in], refs[n_in:2 * n_in]
    res_ref, lng_ref, lnb_ref, o_ref, ob_ref = refs[2 * n_in:]
    for s0 in range(0, o_ref.shape[0], sub):
        rows = slice(s0, s0 + sub)
        y = None
        for a_ref, b_ref in zip(a_refs, b_refs):
            part = jnp.dot(a_ref[rows, :], b_ref[...], preferred_element_type=F32)
            y = part if y is None else y + part
        out = _layer_norm_rows(alpha * res_ref[rows, :] + y, lng_ref[...], lnb_ref[...])
        o_ref[rows, :] = out
        ob_ref[rows, :] = out.astype(BF16)


def matmul_residual_ln(a_list, b_list, res, g, beta, *, tm, sub, alpha, name):
    M, N = res.shape
    assert M % tm == 0 and tm % sub == 0
    row = lambda i: (i, 0)
    const = lambda i: (0, 0)
    in_specs = ([pl.BlockSpec((tm, a.shape[1]), row) for a in a_list]
                + [pl.BlockSpec(b.shape, const, pipeline_mode=pl.Buffered(1)) for b in b_list]
                + [pl.BlockSpec((tm, N), row), pl.BlockSpec((1, N), const),
                   pl.BlockSpec((1, N), const)])
    return pl.pallas_call(
        functools.partial(_mm_res_ln_kernel, n_in=len(a_list), sub=sub, alpha=alpha),
        grid=(M // tm,),
        in_specs=in_specs,
        out_specs=[pl.BlockSpec((tm, N), row), pl.BlockSpec((tm, N), row)],
        out_shape=[jax.ShapeDtypeStruct((M, N), F32), jax.ShapeDtypeStruct((M, N), BF16)],
        compiler_params=_params("parallel"),
        name=name,
    )(*a_list, *b_list, res, g.reshape(1, N), beta.reshape(1, N))


def _t5_bucket(rel):
    half = REL_BUCKETS // 2
    max_exact = half // 2
    n = jnp.abs(rel)
    large = max_exact + (jnp.log(jnp.maximum(n, 1).astype(F32) / max_exact)
                         / math.log(REL_MAX_DIST / max_exact)
                         * (half - max_exact)).astype(jnp.int32)
    large = jnp.minimum(large, half - 1)
    return jnp.where(rel > 0, half, 0) + jnp.where(n < max_exact, n, large)


_ATTN_WINDOW_DELTAS = (0, -BAND_HALF, -2 * BAND_HALF)


def _attn_bias_tiles(table):
    QB, W, N = ATTN_Q_BLOCK, 2 * ATTN_Q_BLOCK, 8 * ATTN_Q_BLOCK
    base = -min(_ATTN_WINDOW_DELTAS)
    j = jnp.arange(N)
    m = jnp.where(j < N // 2, j, j - N) - base
    tiles = []
    for window, dilation in DILATED_BRANCHES:
        assert window // (2 * dilation) == BAND_HALF
        vals = jnp.where((jnp.abs(m) <= BAND_HALF)[:, None],
                         table[_t5_bucket(m * dilation)].astype(F32), NEG_INF)
        flat = jnp.tile(vals.T, (1, QB))[:, :QB * (N - 1)]
        circ = flat.reshape(-1, QB, N - 1)
        tiles.append(jnp.stack([circ[:, :, base + d:base + d + W] for d in _ATTN_WINDOW_DELTAS]))
    return jnp.stack(tiles)


def _band_block(q, k, v, bias):
    s = lax.dot_general(q, k, (((1,), (1,)), ((), ())), preferred_element_type=F32)
    s = s * (ATTN_HEAD_DIM ** -0.5) + bias
    m = jnp.max(s, axis=-1, keepdims=True)
    p = jnp.exp(s - m)
    l = jnp.sum(p, axis=-1, keepdims=True)
    o = jnp.dot(p.astype(BF16), v, preferred_element_type=F32) / l
    return o, m + jnp.log(l)


def _attn_kernel(q_ref, k_ref, v_ref, bias_ref, o_ref, qf, kf, vf, og, lg):
    S = q_ref.shape[0]
    QB = ATTN_Q_BLOCK
    qf[...] = q_ref[...].astype(F32)
    kf[...] = k_ref[...].astype(F32)
    vf[...] = v_ref[...].astype(F32)

    for g, (_, d) in enumerate(DILATED_BRANCHES):
        L = S // d
        assert L % QB == 0 and (L == QB or L >= 3 * QB)
        for r in range(d):
            for n in range(L // QB):
                q0 = n * QB
                if L == QB:
                    ks, W = 0, QB
                else:
                    ks, W = min(max(q0 - BAND_HALF, 0), L - 2 * QB), 2 * QB
                case = _ATTN_WINDOW_DELTAS.index(ks - q0)
                if d == 1:
                    qb = q_ref[q0:q0 + QB, :]
                    kb = k_ref[ks:ks + W, :]
                    vb = v_ref[ks:ks + W, :]
                else:
                    qb = qf[pl.ds(r + d * q0, QB, stride=d), :].astype(BF16)
                    kb = kf[pl.ds(r + d * ks, W, stride=d), :].astype(BF16)
                    vb = vf[pl.ds(r + d * ks, W, stride=d), :].astype(BF16)
                bias = bias_ref[g, case, 0][:, :W]
                o, lse = _band_block(qb, kb, vb, bias)
                lse = jnp.broadcast_to(lse, (QB, LANES))
                if d == 1:
                    og[g, q0:q0 + QB, :] = o
                    lg[g, q0:q0 + QB, :] = lse
                else:
                    og[g, pl.ds(r + d * q0, QB, stride=d), :] = o
                    lg[g, pl.ds(r + d * q0, QB, stride=d), :] = lse

    l0, l1, l2 = lg[0], lg[1], lg[2]
    mx = jnp.maximum(jnp.maximum(l0, l1), l2)
    e0, e1, e2 = jnp.exp(l0 - mx), jnp.exp(l1 - mx), jnp.exp(l2 - mx)
    out = (e0 * og[0] + e1 * og[1] + e2 * og[2]) / (e0 + e1 + e2)
    o_ref[...] = out.astype(o_ref.dtype)


def attention(proj, bias_tiles, *, batch, seq):
    H, Dh = ATTN_HEADS, ATTN_HEAD_DIM
    assert Dh == LANES
    T = batch * seq
    blk = lambda off: pl.BlockSpec((seq, Dh), lambda h, b: (b, off + h))
    return pl.pallas_call(
        _attn_kernel,
        grid=(H, batch),
        in_specs=[blk(0), blk(H), blk(2 * H),
                  pl.BlockSpec((3, 3, 1, ATTN_Q_BLOCK, 2 * ATTN_Q_BLOCK),
                               lambda h, b: (0, 0, h, 0, 0))],
        out_specs=pl.BlockSpec((seq, Dh), lambda h, b: (b, h)),
        out_shape=jax.ShapeDtypeStruct((T, H * Dh), BF16),
        scratch_shapes=[pltpu.VMEM((seq, Dh), F32)] * 3
                       + [pltpu.VMEM((3, seq, Dh), F32)] * 2,
        compiler_params=_params("parallel", "parallel"),
        name="dilated_attention",
    )(proj, proj, proj, bias_tiles)


def _log_sigmoid(x):
    return jnp.minimum(x, 0.0) - jnp.log1p(jnp.exp(-jnp.abs(x)))


def _scan_lanes(x, combine, identity, reverse):
    n = x.shape[-1]
    axis = x.ndim - 1
    lane = lax.broadcasted_iota(jnp.int32, x.shape, axis)
    k = 1
    while k < n:
        if reverse:
            shifted = jnp.where(lane < n - k, pltpu.roll(x, n - k, axis=axis), identity)
        else:
            shifted = jnp.where(lane >= k, pltpu.roll(x, k, axis=axis), identity)
        x = combine(x, shifted)
        k *= 2
    return x


def _mlstm_kernel(mq_ref, mk_ref, mv_ref, mo_ref, cwq_ref, cwk_ref, gr_ref, gb_ref, o_ref,
                  qs, ktf, ktb, vaug, hs, ct, colv):
    S, Dh = mq_ref.shape
    C = MLSTM_CHUNK
    nc = S // C
    halo = CONV_WIDTH // 2
    lane_s = lax.broadcasted_iota(jnp.int32, (Dh, S), 1)

    def conv_silu_t(src_ref, wt_ref):
        xt = src_ref[...].astype(F32).T
        wt = wt_ref[...]
        acc = xt * wt[:, halo:halo + 1]
        for j in range(CONV_WIDTH):
            off = j - halo
            if off == 0:
                continue
            shifted = pltpu.roll(xt, (-off) % S, axis=1)
            ok = (lane_s < S - off) if off > 0 else (lane_s >= -off)
            acc = acc + jnp.where(ok, shifted, 0.0) * wt[:, j:j + 1]
        return acc * jax.nn.sigmoid(acc)

    qs[...] = conv_silu_t(mq_ref, cwq_ref).T.astype(BF16)
    kt = conv_silu_t(mk_ref, cwk_ref) * (Dh ** -0.5)
    ktf[...] = kt
    ktb[...] = kt.astype(BF16)
    vaug[:, :Dh] = mv_ref[...]
    vaug[:, Dh:] = jnp.ones((S, LANES), BF16)

    gb = gb_ref[0]
    per_dir, col_rows = [], []
    for di in range(2):
        reverse = di == 1
        gi = 2 * di
        log_i = gr_ref[0, 0, gi] + gb[gi:gi + 1, :]
        log_f = _log_sigmoid(gr_ref[0, 0, gi + 1] + gb[gi + 1:gi + 2, :])
        b = _scan_lanes(log_f, jnp.add, 0.0, reverse)
        a = log_i - b
        cm = _scan_lanes(a, jnp.maximum, -jnp.inf, reverse)
        per_dir.append((a, b))
        col_rows += [cm, b]
    pad = jnp.zeros((LANES - 4 * nc, C), F32)
    colv[...] = jnp.concatenate(col_rows + [pad], axis=0).T

    ti = lax.broadcasted_iota(jnp.int32, (C, C), 0)
    si = lax.broadcasted_iota(jnp.int32, (C, C), 1)
    for di in range(2):
        forward = di == 0
        a_all, b_all = per_dir[di]
        feeds = (si <= ti) if forward else (si >= ti)
        ct[...] = jnp.zeros_like(ct)
        m_prev = jnp.zeros((1, 1), F32)
        for step in range(nc):
            c = step if forward else nc - 1 - step
            rows = slice(c * C, (c + 1) * C)
            q = qs[rows, :]
            va = vaug[rows, :]
            a_row = a_all[c:c + 1, :]
            b_last = b_all[c:c + 1, C - 1:C] if forward else b_all[c:c + 1, 0:1]
            col = 2 * nc * di + c
            m_col = jnp.maximum(colv[:, col:col + 1], m_prev)
            b_col = colv[:, col + nc:col + nc + 1]
            e = jnp.exp(jnp.where(feeds, a_row - m_col, -jnp.inf))
            s = jnp.dot(q, ktb[:, rows], preferred_element_type=F32)
            sc = (s * e).astype(BF16)
            w_inter = jnp.exp(m_prev - m_col)
            numa = (jnp.dot(sc, va, preferred_element_type=F32)
                    + w_inter * jnp.dot(q, ct[...].astype(BF16), preferred_element_type=F32))
            den = numa[:, Dh:Dh + 1]
            inv = 1.0 / jnp.maximum(jnp.abs(den), jnp.exp(-(b_col + m_col)))
            h = numa[:, :Dh] * inv
            if forward:
                hs[rows, :] = h
            else:
                hs[rows, :] += h
            if step < nc - 1:
                g_r = b_last + a_row
                m_new = jnp.maximum(b_last + m_prev, jnp.max(g_r, axis=1, keepdims=True))
                decay = jnp.exp(b_last + m_prev - m_new)
                ktw = (ktf[:, rows] * jnp.exp(g_r - m_new)).astype(BF16)
                ct[...] = decay * ct[...] + jnp.dot(ktw, va, preferred_element_type=F32)
                m_prev = m_new

    o_ref[...] = (jax.nn.sigmoid(mo_ref[...].astype(F32)) * hs[...]).astype(o_ref.dtype)


def mlstm(proj, gates, conv_w, gate_b, *, batch, seq, col0):
    H, Dh, C = MLSTM_HEADS, MLSTM_HEAD_DIM, MLSTM_CHUNK
    T = batch * seq
    nc = seq // C
    assert 4 * nc <= LANES
    c0 = col0 // Dh
    g_row = gates[:, :4 * H].reshape(batch, seq, 4, H).transpose(0, 3, 2, 1)
    g_row = g_row.reshape(batch, H, 4, nc, C)
    gb = gate_b.reshape(4, H).T.reshape(H, 4, 1)
    cw_t = conv_w.T
    blk = lambda j: pl.BlockSpec((seq, Dh), lambda b, h: (b, c0 + j * H + h))
    return pl.pallas_call(
        _mlstm_kernel,
        grid=(batch, H),
        in_specs=[blk(0), blk(1), blk(2), blk(3),
                  pl.BlockSpec((Dh, CONV_WIDTH), lambda b, h: (h, 0)),
                  pl.BlockSpec((Dh, CONV_WIDTH), lambda b, h: (H + h, 0)),
                  pl.BlockSpec((1, 1, 4, nc, C), lambda b, h: (b, h, 0, 0, 0)),
                  pl.BlockSpec((1, 4, 1), lambda b, h: (h, 0, 0))],
        out_specs=pl.BlockSpec((seq, Dh), lambda b, h: (b, h)),
        out_shape=jax.ShapeDtypeStruct((T, H * Dh), BF16),
        scratch_shapes=[pltpu.VMEM((seq, Dh), BF16),
                        pltpu.VMEM((Dh, seq), F32),
                        pltpu.VMEM((Dh, seq), BF16),
                        pltpu.VMEM((seq, Dh + LANES), BF16),
                        pltpu.VMEM((seq, Dh), F32),
                        pltpu.VMEM((Dh, Dh + LANES), F32),
                        pltpu.VMEM((C, LANES), F32)],
        compiler_params=_params("parallel", "parallel"),
        name="bidir_mlstm",
    )(proj, proj, proj, proj, cw_t, cw_t, g_row, gb)


def _router_kernel(x_ref, wr_ref, br_ref, g_ref, rank_ref, cnt_ref):
    tm = x_ref.shape[0]

    @pl.when(pl.program_id(0) == 0)
    def _():
        cnt_ref[...] = jnp.zeros_like(cnt_ref)

    logits = jnp.dot(x_ref[...], wr_ref[...], preferred_element_type=F32,
                     precision=lax.Precision.HIGHEST) + br_ref[...]
    lane = lax.broadcasted_iota(jnp.int32, logits.shape, 1)
    lg = jnp.where(lane < N_EXPERTS, logits, -jnp.inf)
    v1 = jnp.max(lg, axis=1, keepdims=True)
    i1 = jnp.min(jnp.where(lg == v1, lane, LANES), axis=1, keepdims=True)
    lg2 = jnp.where(lane == i1, -jnp.inf, lg)
    v2 = jnp.max(lg2, axis=1, keepdims=True)
    i2 = jnp.min(jnp.where(lg2 == v2, lane, LANES), axis=1, keepdims=True)
    e = jnp.exp(v2 - v1)
    p1 = 1.0 / (1.0 + e)
    p2 = e / (1.0 + e)
    g_ref[...] = jnp.where(lane == i1, p1, 0.0) + jnp.where(lane == i2, p2, 0.0)
    sel = (lane == i1) | (lane == i2)
    self_ = jnp.where(sel, 1.0, 0.0)
    row = lax.broadcasted_iota(jnp.int32, (tm, tm), 0)
    col = lax.broadcasted_iota(jnp.int32, (tm, tm), 1)
    earlier = jnp.where(col < row, 1.0, 0.0).astype(BF16)
    raw = (jnp.dot(earlier, self_.astype(BF16), preferred_element_type=F32)
           + cnt_ref[...]).astype(jnp.int32)
    rank_ref[...] = jnp.where(sel, raw, -1 - raw)
    cnt_ref[...] += jnp.sum(self_, axis=0, keepdims=True)


def router(x, wr, br, *, tm):
    T, D = x.shape
    wr_p = jnp.zeros((D, LANES), F32).at[:, :N_EXPERTS].set(wr)
    br_p = jnp.zeros((1, LANES), F32).at[0, :N_EXPERTS].set(br)
    return pl.pallas_call(
        _router_kernel,
        grid=(T // tm,),
        in_specs=[pl.BlockSpec((tm, D), lambda i: (i, 0)),
                  pl.BlockSpec((D, LANES), lambda i: (0, 0)),
                  pl.BlockSpec((1, LANES), lambda i: (0, 0))],
        out_specs=[pl.BlockSpec((tm, LANES), lambda i: (i, 0)),
                   pl.BlockSpec((tm, LANES), lambda i: (i, 0))],
        out_shape=[jax.ShapeDtypeStruct((T, LANES), F32),
                   jax.ShapeDtypeStruct((T, LANES), jnp.int32)],
        scratch_shapes=[pltpu.VMEM((1, LANES), F32)],
        compiler_params=_params("arbitrary"),
        name="moe_router",
    )(x, wr_p, br_p)


MOE_TILE = 256
MOE_TOKEN_BLOCK = 512


def _moe_plan(rank, *, tile, tb):
    T = rank.shape[0]
    E = N_EXPERTS
    nb = T // tb
    n_tiles = 2 * T // tile + E
    rows_max = n_tiles * tile
    r = rank[:, :E]
    sel = r >= 0
    raw = jnp.where(sel, r, -1 - r)
    cb = raw[::tb]
    counts = raw[-1] + sel[-1].astype(jnp.int32)
    padded = (counts + tile - 1) // tile * tile
    ends = jnp.cumsum(padded)
    start = ends - padded
    pos = jnp.where(sel, start[None, :] + r, -1)
    pos_pad = jnp.full((T, LANES), -1, jnp.int32).at[:, :E].set(pos)
    lo = (start[None, :] + cb).T.reshape(-1)
    hi = (start[None, :] + jnp.concatenate([cb[1:], counts[None, :]], axis=0)).T.reshape(-1)
    tile_starts = jnp.arange(n_tiles, dtype=jnp.int32) * tile
    bounds = jnp.sort(jnp.concatenate([tile_starts, lo]))
    nxt = jnp.concatenate([bounds[1:], jnp.array([rows_max], jnp.int32)])
    count_le = lambda keys, x: jnp.sum((keys[None, :] <= x[:, None]).astype(jnp.int32), axis=1)
    pair = jnp.clip(count_le(lo, bounds) - 1, 0, E * nb - 1)
    seg_hi = jnp.minimum(nxt, hi[pair])
    seg_valid = (seg_hi > bounds).astype(jnp.int32)
    seg_tile = jnp.minimum(bounds // tile, n_tiles - 1).astype(jnp.int32)
    seg_e = (pair // nb).astype(jnp.int32)
    seg_b = (pair % nb).astype(jnp.int32)
    changed = lambda a: jnp.concatenate([jnp.ones((1,), jnp.int32),
                                         (a[1:] != a[:-1]).astype(jnp.int32)])
    gather_plan = (seg_tile, seg_b, seg_e, seg_valid, changed(seg_tile))
    n_seg = bounds.shape[0]
    order = jnp.argsort(seg_b * n_seg + jnp.arange(n_seg, dtype=jnp.int32))
    cb_, ct_, ce_, cv_ = seg_b[order], seg_tile[order], seg_e[order], seg_valid[order]
    first = changed(cb_)
    last = jnp.concatenate([first[1:], jnp.ones((1,), jnp.int32)])
    combine_plan = (ct_, cb_, ce_, cv_, first, last)
    tile_expert = jnp.clip(count_le(ends, tile_starts), 0, E - 1).astype(jnp.int32)
    tile_valid = (tile_starts < ends[-1]).astype(jnp.int32)
    return pos_pad, gather_plan, combine_plan, tile_expert, tile_valid, n_tiles


def _gather_kernel(seg_tile, seg_blk, seg_e, seg_valid, seg_first, x_ref, post_ref, o_ref):
    s = pl.program_id(0)
    tile, tb = o_ref.shape[0], x_ref.shape[0]

    @pl.when(seg_first[s] == 1)
    def _():
        o_ref[...] = jnp.zeros_like(o_ref)

    @pl.when(seg_valid[s] == 1)
    def _():
        rel = post_ref[pl.ds(seg_e[s], 1), :] - seg_tile[s] * tile
        rid = lax.broadcasted_iota(jnp.int32, (tile, tb), 0)
        onehot = jnp.where(rel == rid, 1.0, 0.0).astype(BF16)
        o_ref[...] += jnp.dot(onehot, x_ref[...], preferred_element_type=F32).astype(BF16)


def moe_gather(xb, pos_t, plan, *, n_tiles, tile, tb):
    T, D = xb.shape
    n_seg = plan[0].shape[0]
    grid_spec = pltpu.PrefetchScalarGridSpec(
        num_scalar_prefetch=5, grid=(n_seg,),
        in_specs=[pl.BlockSpec((tb, D), lambda s, st, sb, se, sv, sf: (sb[s], 0)),
                  pl.BlockSpec((N_EXPERTS, tb), lambda s, st, sb, se, sv, sf: (0, sb[s]))],
        out_specs=pl.BlockSpec((tile, D), lambda s, st, sb, se, sv, sf: (st[s], 0)))
    return pl.pallas_call(
        _gather_kernel, grid_spec=grid_spec,
        out_shape=jax.ShapeDtypeStruct((n_tiles * tile, D), BF16),
        compiler_params=_params("arbitrary"), name="moe_gather",
    )(*plan, xb, pos_t)


def _expert_swiglu_kernel(te, tv, x_ref, w1_ref, w3_ref, o_ref):
    i = pl.program_id(1)

    @pl.when(tv[i] == 1)
    def _():
        a = x_ref[...]
        h1 = jnp.dot(a, w1_ref[0], preferred_element_type=F32)
        h3 = jnp.dot(a, w3_ref[0], preferred_element_type=F32)
        o_ref[...] = (h1 * jax.nn.sigmoid(h1) * h3).astype(o_ref.dtype)

    @pl.when(tv[i] == 0)
    def _():
        o_ref[...] = jnp.zeros_like(o_ref)


def expert_swiglu(xs, w1, w3, tile_expert, tile_valid, *, tile, tf):
    R, D = xs.shape
    E, _, F = w1.shape
    n_tiles = R // tile
    grid_spec = pltpu.PrefetchScalarGridSpec(
        num_scalar_prefetch=2, grid=(F // tf, n_tiles),
        in_specs=[pl.BlockSpec((tile, D), lambda f, i, te, tv: (i, 0)),
                  pl.BlockSpec((1, D, tf), lambda f, i, te, tv: (te[i], 0, f)),
                  pl.BlockSpec((1, D, tf), lambda f, i, te, tv: (te[i], 0, f))],
        out_specs=pl.BlockSpec((tile, tf), lambda f, i, te, tv: (i, f)))
    return pl.pallas_call(
        _expert_swiglu_kernel, grid_spec=grid_spec,
        out_shape=jax.ShapeDtypeStruct((R, F), BF16),
        compiler_params=_params("parallel", "parallel"), name="expert_swiglu",
    )(tile_expert, tile_valid, xs, w1, w3)


def _expert_w2_kernel(te, tv, h_ref, w2_ref, o_ref):
    i = pl.program_id(0)

    @pl.when(tv[i] == 1)
    def _():
        o_ref[...] = jnp.dot(h_ref[...], w2_ref[0], preferred_element_type=F32).astype(o_ref.dtype)

    @pl.when(tv[i] == 0)
    def _():
        o_ref[...] = jnp.zeros_like(o_ref)


def expert_w2(h, w2, tile_expert, tile_valid, *, tile):
    R, F = h.shape
    D = w2.shape[2]
    grid_spec = pltpu.PrefetchScalarGridSpec(
        num_scalar_prefetch=2, grid=(R // tile,),
        in_specs=[pl.BlockSpec((tile, F), lambda i, te, tv: (i, 0)),
                  pl.BlockSpec((1, F, D), lambda i, te, tv: (te[i], 0, 0))],
        out_specs=pl.BlockSpec((tile, D), lambda i, te, tv: (i, 0)))
    return pl.pallas_call(
        _expert_w2_kernel, grid_spec=grid_spec,
        out_shape=jax.ShapeDtypeStruct((R, D), BF16),
        compiler_params=_params("parallel"), name="expert_w2",
    )(tile_expert, tile_valid, h, w2)


def _combine_kernel(seg_tile, seg_blk, seg_e, seg_valid, seg_first, seg_last,
                    y_ref, pos_ref, g_ref, res_ref, lng_ref, lnb_ref, o_ref, ob_ref, acc_ref,
                    *, alpha):
    s = pl.program_id(0)
    tile, tb = y_ref.shape[0], pos_ref.shape[0]

    @pl.when(seg_first[s] == 1)
    def _():
        acc_ref[...] = jnp.zeros_like(acc_ref)

    @pl.when(seg_valid[s] == 1)
    def _():
        lane = lax.broadcasted_iota(jnp.int32, (tb, LANES), 1)
        mine = lane == seg_e[s]
        pcol = jnp.sum(jnp.where(mine, pos_ref[...], 0.0), axis=1, keepdims=True)
        gcol = jnp.sum(jnp.where(mine, g_ref[...], 0.0), axis=1, keepdims=True)
        rel = pcol - (seg_tile[s] * tile).astype(F32)
        cid = lax.broadcasted_iota(jnp.int32, (tb, tile), 1).astype(F32)
        weights = jnp.where(rel == cid, gcol, 0.0).astype(BF16)
        acc_ref[...] += jnp.dot(weights, y_ref[...], preferred_element_type=F32)

    @pl.when(seg_last[s] == 1)
    def _():
        out = _layer_norm_rows(alpha * res_ref[...] + acc_ref[...], lng_ref[...], lnb_ref[...])
        o_ref[...] = out
        ob_ref[...] = out.astype(BF16)


def moe_combine_ln(y, pos_f, gates, res, g, beta, plan, *, tile, tb, alpha):
    T, D = res.shape
    n_seg = plan[0].shape[0]
    tok = lambda s, st, sb, se, sv, sf, sl: (sb[s], 0)
    const = lambda s, st, sb, se, sv, sf, sl: (0, 0)
    grid_spec = pltpu.PrefetchScalarGridSpec(
        num_scalar_prefetch=6, grid=(n_seg,),
        in_specs=[pl.BlockSpec((tile, D), lambda s, st, sb, se, sv, sf, sl: (st[s], 0)),
                  pl.BlockSpec((tb, LANES), tok), pl.BlockSpec((tb, LANES), tok),
                  pl.BlockSpec((tb, D), tok),
                  pl.BlockSpec((1, D), const), pl.BlockSpec((1, D), const)],
        out_specs=[pl.BlockSpec((tb, D), tok), pl.BlockSpec((tb, D), tok)],
        scratch_shapes=[pltpu.VMEM((tb, D), F32)])
    return pl.pallas_call(
        functools.partial(_combine_kernel, alpha=alpha), grid_spec=grid_spec,
        out_shape=[jax.ShapeDtypeStruct((T, D), F32), jax.ShapeDtypeStruct((T, D), BF16)],
        compiler_params=_params("arbitrary"), name="moe_combine_ln",
    )(*plan, y, pos_f, gates, res, g.reshape(1, D), beta.reshape(1, D))


def moe_layer(xf, xb, wr, br, w1, w3, w2, ln_g, ln_b, *, alpha):
    tile, tb = MOE_TILE, MOE_TOKEN_BLOCK
    gates, rank = router(xf, wr, br, tm=1024)
    pos, gather_plan, combine_plan, tile_expert, tile_valid, n_tiles = _moe_plan(rank, tile=tile, tb=tb)
    xs = moe_gather(xb, pos[:, :N_EXPERTS].T, gather_plan, n_tiles=n_tiles, tile=tile, tb=tb)
    h = expert_swiglu(xs, w1, w3, tile_expert, tile_valid, tile=tile, tf=1408)
    y = expert_w2(h, w2, tile_expert, tile_valid, tile=tile)
    return moe_combine_ln(y, pos.astype(F32), gates, xf, ln_g, ln_b, combine_plan,
                          tile=tile, tb=tb, alpha=alpha)


def kernel(x, w_in, w_out, conv_w, gate_b, rpb_table, ln_g, ln_b, dense_w1, dense_w3,
           dense_w2, router_w, router_b, moe_w1, moe_w3, moe_w2):
    B, S, D = x.shape
    T = B * S
    depth = w_in.shape[0]
    alpha = (2 * depth) ** 0.25
    attn_w = ATTN_HEADS * ATTN_HEAD_DIM
    mlstm_w = MLSTM_HEADS * MLSTM_HEAD_DIM
    main_cols = 3 * attn_w + 4 * mlstm_w

    bias_tiles = _attn_bias_tiles(rpb_table)
    xf = x.reshape(T, D)
    xb = xf.astype(BF16)
    for l in range(depth):
        w_main = w_in[l, :, :main_cols].astype(BF16)
        w_gate = jnp.pad(w_in[l, :, main_cols:], ((0, 0), (0, LANES - 4 * MLSTM_HEADS))).astype(BF16)
        proj = matmul(xb, w_main, tm=1024, tn=1024, out_dtype=BF16, name="in_proj")
        gates = matmul(xb, w_gate, tm=2048, tn=LANES, out_dtype=F32, name="gate_proj")
        attn = attention(proj, bias_tiles, batch=B, seq=S)
        rec = mlstm(proj, gates, conv_w[l], gate_b[l], batch=B, seq=S, col0=3 * attn_w)
        wo = w_out[l].astype(BF16)
        xf, xb = matmul_residual_ln([attn, rec], [wo[:attn_w], wo[attn_w:]], xf,
                                    ln_g[l, 0], ln_b[l, 0], tm=512, sub=256, alpha=alpha,
                                    name="out_proj_ln")
        j = l // 2
        if l % 2 == 0:
            hmid = swiglu_matmul(xb, dense_w1[j].astype(BF16), dense_w3[j].astype(BF16),
                                 tm=1024, tf=1408, name="dense_swiglu")
            xf, xb = matmul_residual_ln([hmid], [dense_w2[j].astype(BF16)], xf,
                                        ln_g[l, 1], ln_b[l, 1], tm=256, sub=256, alpha=alpha,
                                        name="dense_w2_ln")
        else:
            xf, xb = moe_layer(xf, xb, router_w[j], router_b[j], moe_w1[j].astype(BF16),
                               moe_w3[j].astype(BF16), moe_w2[j].astype(BF16),
                               ln_g[l, 1], ln_b[l, 1], alpha=alpha)
    return xf.reshape(B, S, D)
```

```python
import functools
import math

import jax
import jax.numpy as jnp
from jax import lax
from jax.experimental import pallas as pl
from jax.experimental.pallas import tpu as pltpu

F32 = jnp.float32
BF16 = jnp.bfloat16

ATTN_HEADS = 8
ATTN_HEAD_DIM = 128
MLSTM_HEADS = 4
MLSTM_HEAD_DIM = 256
DILATED_BRANCHES = ((128, 1), (512, 4), (2048, 16))
BAND_HALF = 64
REL_BUCKETS = 32
REL_MAX_DIST = 1024
NEG_INF = -1e30
CONV_WIDTH = 5
N_EXPERTS = 8
LN_EPS = 1e-5

V7X_VMEM_BYTES = 64 * 1024 * 1024
VMEM_LIMIT_BYTES = V7X_VMEM_BYTES - 8 * 1024 * 1024
LANES = 128

ATTN_Q_BLOCK = 128
MLSTM_CHUNK = 256


def _params(*semantics):
    return pltpu.CompilerParams(dimension_semantics=semantics,
                                vmem_limit_bytes=VMEM_LIMIT_BYTES)


def _mm_kernel(a_ref, b_ref, o_ref):
    o_ref[...] = jnp.dot(a_ref[...], b_ref[...],
                         preferred_element_type=F32).astype(o_ref.dtype)


def matmul(a, b, *, tm, tn, out_dtype, name):
    M, K = a.shape
    N = b.shape[1]
    assert M % tm == 0 and N % tn == 0
    return pl.pallas_call(
        _mm_kernel,
        grid=(M // tm, N // tn),
        in_specs=[pl.BlockSpec((tm, K), lambda i, j: (i, 0)),
                  pl.BlockSpec((K, tn), lambda i, j: (0, j))],
        out_specs=pl.BlockSpec((tm, tn), lambda i, j: (i, j)),
        out_shape=jax.ShapeDtypeStruct((M, N), out_dtype),
        compiler_params=_params("parallel", "parallel"),
        name=name,
    )(a, b)


def _in_proj_kernel(a_ref, w_ref, o_ref, wb_ref):
    @pl.when(pl.program_id(1) == 0)
    def _():
        wb_ref[...] = w_ref[0].astype(BF16)

    o_ref[...] = jnp.dot(a_ref[...], wb_ref[...],
                         preferred_element_type=F32).astype(o_ref.dtype)


def in_projection(a, w_stack, layer, *, n_cols, tm, tn):
    M, K = a.shape
    assert M % tm == 0 and n_cols % tn == 0
    return pl.pallas_call(
        _in_proj_kernel,
        grid=(n_cols // tn, M // tm),
        in_specs=[pl.BlockSpec((tm, K), lambda j, i: (i, 0)),
                  pl.BlockSpec((1, K, tn), lambda j, i: (layer, 0, j))],
        out_specs=pl.BlockSpec((tm, tn), lambda j, i: (i, j)),
        out_shape=jax.ShapeDtypeStruct((M, n_cols), BF16),
        scratch_shapes=[pltpu.VMEM((K, tn), BF16)],
        compiler_params=_params("parallel", "arbitrary"),
        name="in_proj",
    )(a, w_stack)


def _swiglu_kernel(a_ref, w1_ref, w3_ref, o_ref):
    a = a_ref[...]
    h1 = jnp.dot(a, w1_ref[...], preferred_element_type=F32)
    h3 = jnp.dot(a, w3_ref[...], preferred_element_type=F32)
    o_ref[...] = (h1 * jax.nn.sigmoid(h1) * h3).astype(o_ref.dtype)


def swiglu_matmul(a, w1, w3, *, tm, tf, name):
    M, K = a.shape
    F = w1.shape[1]
    assert M % tm == 0 and F % tf == 0
    return pl.pallas_call(
        _swiglu_kernel,
        grid=(M // tm, F // tf),
        in_specs=[pl.BlockSpec((tm, K), lambda i, j: (i, 0)),
                  pl.BlockSpec((K, tf), lambda i, j: (0, j)),
                  pl.BlockSpec((K, tf), lambda i, j: (0, j))],
        out_specs=pl.BlockSpec((tm, tf), lambda i, j: (i, j)),
        out_shape=jax.ShapeDtypeStruct((M, F), BF16),
        compiler_params=_params("parallel", "parallel"),
        name=name,
    )(a, w1, w3)


def _layer_norm_rows(z, g, b):
    mu = jnp.mean(z, axis=-1, keepdims=True)
    zc = z - mu
    var = jnp.mean(zc * zc, axis=-1, keepdims=True)
    return zc * lax.rsqrt(var + LN_EPS) * g + b


def _mm_res_ln_kernel(*refs, n_in, sub, alpha):
    a_refs, b_refs = refs[:n_in], refs[n_in:2 * n_in]
    res_ref, lng_ref, lnb_ref, o_ref, ob_ref = refs[2 * n_in:]
    for s0 in range(0, o_ref.shape[0], sub):
        rows = slice(s0, s0 + sub)
        y = None
        for a_ref, b_ref in zip(a_refs, b_refs):
            part = jnp.dot(a_ref[rows, :], b_ref[...], preferred_element_type=F32)
            y = part if y is None else y + part
        out = _layer_norm_rows(alpha * res_ref[rows, :] + y, lng_ref[...], lnb_ref[...])
        o_ref[rows, :] = out
        ob_ref[rows, :] = out.astype(BF16)


def matmul_residual_ln(a_list, b_list, res, g, beta, *, tm, sub, alpha, name):
    M, N = res.shape
    assert M % tm == 0 and tm % sub == 0
    row = lambda i: (i, 0)
    const = lambda i: (0, 0)
    in_specs = ([pl.BlockSpec((tm, a.shape[1]), row) for a in a_list]
                + [pl.BlockSpec(b.shape, const, pipeline_mode=pl.Buffered(1)) for b in b_list]
                + [pl.BlockSpec((tm, N), row), pl.BlockSpec((1, N), const),
                   pl.BlockSpec((1, N), const)])
    return pl.pallas_call(
        functools.partial(_mm_res_ln_kernel, n_in=len(a_list), sub=sub, alpha=alpha),
        grid=(M // tm,),
        in_specs=in_specs,
        out_specs=[pl.BlockSpec((tm, N), row), pl.BlockSpec((tm, N), row)],
        out_shape=[jax.ShapeDtypeStruct((M, N), F32), jax.ShapeDtypeStruct((M, N), BF16)],
        compiler_params=_params("parallel"),
        name=name,
    )(*a_list, *b_list, res, g.reshape(1, N), beta.reshape(1, N))


def _t5_bucket(rel):
    half = REL_BUCKETS // 2
    max_exact = half // 2
    n = jnp.abs(rel)
    large = max_exact + (jnp.log(jnp.maximum(n, 1).astype(F32) / max_exact)
                         / math.log(REL_MAX_DIST / max_exact)
                         * (half - max_exact)).astype(jnp.int32)
    large = jnp.minimum(large, half - 1)
    return jnp.where(rel > 0, half, 0) + jnp.where(n < max_exact, n, large)


_ATTN_WINDOW_DELTAS = (0, -BAND_HALF, -2 * BAND_HALF)


def _attn_bias_tiles(table):
    QB, W, N = ATTN_Q_BLOCK, 2 * ATTN_Q_BLOCK, 8 * ATTN_Q_BLOCK
    base = -min(_ATTN_WINDOW_DELTAS)
    j = jnp.arange(N)
    m = jnp.where(j < N // 2, j, j - N) - base
    tiles = []
    for window, dilation in DILATED_BRANCHES:
        assert window // (2 * dilation) == BAND_HALF
        vals = jnp.where((jnp.abs(m) <= BAND_HALF)[:, None],
                         table[_t5_bucket(m * dilation)].astype(F32), NEG_INF)
        flat = jnp.tile(vals.T, (1, QB))[:, :QB * (N - 1)]
        circ = flat.reshape(-1, QB, N - 1)
        tiles.append(jnp.stack([circ[:, :, base + d:base + d + W] for d in _ATTN_WINDOW_DELTAS]))
    return jnp.stack(tiles)


def _band_block(q, k, v, bias):
    s = lax.dot_general(q, k, (((1,), (1,)), ((), ())), preferred_element_type=F32)
    s = s * (ATTN_HEAD_DIM ** -0.5) + bias
    m = jnp.max(s, axis=-1, keepdims=True)
    p = jnp.exp(s - m)
    l = jnp.sum(p, axis=-1, keepdims=True)
    o = jnp.dot(p.astype(BF16), v, preferred_element_type=F32) / l
    return o, m + jnp.log(l)


def _attn_kernel(q_ref, k_ref, v_ref, bias_ref, o_ref, qf, kf, vf, og, lg):
    S = q_ref.shape[0]
    QB = ATTN_Q_BLOCK
    qf[...] = q_ref[...].astype(F32)
    kf[...] = k_ref[...].astype(F32)
    vf[...] = v_ref[...].astype(F32)

    for g, (_, d) in enumerate(DILATED_BRANCHES):
        L = S // d
        assert L % QB == 0 and (L == QB or L >= 3 * QB)
        for r in range(d):
            for n in range(L // QB):
                q0 = n * QB
                if L == QB:
                    ks, W = 0, QB
                else:
                    ks, W = min(max(q0 - BAND_HALF, 0), L - 2 * QB), 2 * QB
                case = _ATTN_WINDOW_DELTAS.index(ks - q0)
                if d == 1:
                    qb = q_ref[q0:q0 + QB, :]
                    kb = k_ref[ks:ks + W, :]
                    vb = v_ref[ks:ks + W, :]
                else:
                    qb = qf[pl.ds(r + d * q0, QB, stride=d), :].astype(BF16)
                    kb = kf[pl.ds(r + d * ks, W, stride=d), :].astype(BF16)
                    vb = vf[pl.ds(r + d * ks, W, stride=d), :].astype(BF16)
                bias = bias_ref[g, case, 0][:, :W]
                o, lse = _band_block(qb, kb, vb, bias)
                lse = jnp.broadcast_to(lse, (QB, LANES))
                if d == 1:
                    og[g, q0:q0 + QB, :] = o
                    lg[g, q0:q0 + QB, :] = lse
                else:
                    og[g, pl.ds(r + d * q0, QB, stride=d), :] = o
                    lg[g, pl.ds(r + d * q0, QB, stride=d), :] = lse

    l0, l1, l2 = lg[0], lg[1], lg[2]
    mx = jnp.maximum(jnp.maximum(l0, l1), l2)
    e0, e1, e2 = jnp.exp(l0 - mx), jnp.exp(l1 - mx), jnp.exp(l2 - mx)
    out = (e0 * og[0] + e1 * og[1] + e2 * og[2]) / (e0 + e1 + e2)
    o_ref[...] = out.astype(o_ref.dtype)


def attention(proj, bias_tiles, *, batch, seq):
    H, Dh = ATTN_HEADS, ATTN_HEAD_DIM
    assert Dh == LANES
    T = batch * seq
    blk = lambda off: pl.BlockSpec((seq, Dh), lambda h, b: (b, off + h))
    return pl.pallas_call(
        _attn_kernel,
        grid=(H, batch),
        in_specs=[blk(0), blk(H), blk(2 * H),
                  pl.BlockSpec((3, 3, 1, ATTN_Q_BLOCK, 2 * ATTN_Q_BLOCK),
                               lambda h, b: (0, 0, h, 0, 0))],
        out_specs=pl.BlockSpec((seq, Dh), lambda h, b: (b, h)),
        out_shape=jax.ShapeDtypeStruct((T, H * Dh), BF16),
        scratch_shapes=[pltpu.VMEM((seq, Dh), F32)] * 3
                       + [pltpu.VMEM((3, seq, Dh), F32)] * 2,
        compiler_params=_params("parallel", "parallel"),
        name="dilated_attention",
    )(proj, proj, proj, bias_tiles)


def _log_sigmoid(x):
    return jnp.minimum(x, 0.0) - jnp.log1p(jnp.exp(-jnp.abs(x)))


def _scan_lanes(x, combine, identity, reverse):
    n = x.shape[-1]
    axis = x.ndim - 1
    lane = lax.broadcasted_iota(jnp.int32, x.shape, axis)
    k = 1
    while k < n:
        if reverse:
            shifted = jnp.where(lane < n - k, pltpu.roll(x, n - k, axis=axis), identity)
        else:
            shifted = jnp.where(lane >= k, pltpu.roll(x, k, axis=axis), identity)
        x = combine(x, shifted)
        k *= 2
    return x


def _mlstm_kernel(mq_ref, mk_ref, mv_ref, mo_ref, cwq_ref, cwk_ref, gr_ref, gb_ref, o_ref,
                  qs, ktf, ktb, vaug, hs, ct, colv):
    S, Dh = mq_ref.shape
    C = MLSTM_CHUNK
    nc = S // C
    halo = CONV_WIDTH // 2
    lane_s = lax.broadcasted_iota(jnp.int32, (Dh, S), 1)

    def conv_silu_t(src_ref, wt_ref):
        xt = src_ref[...].astype(F32).T
        wt = wt_ref[...]
        acc = xt * wt[:, halo:halo + 1]
        for j in range(CONV_WIDTH):
            off = j - halo
            if off == 0:
                continue
            shifted = pltpu.roll(xt, (-off) % S, axis=1)
            ok = (lane_s < S - off) if off > 0 else (lane_s >= -off)
            acc = acc + jnp.where(ok, shifted, 0.0) * wt[:, j:j + 1]
        return acc * jax.nn.sigmoid(acc)

    qs[...] = conv_silu_t(mq_ref, cwq_ref).T.astype(BF16)
    kt = conv_silu_t(mk_ref, cwk_ref) * (Dh ** -0.5)
    ktf[...] = kt
    ktb[...] = kt.astype(BF16)
    vaug[:, :Dh] = mv_ref[...]
    vaug[:, Dh:] = jnp.ones((S, LANES), BF16)

    gb = gb_ref[0]
    per_dir, col_rows = [], []
    for di in range(2):
        reverse = di == 1
        gi = 2 * di
        log_i = gr_ref[0, 0, gi] + gb[gi:gi + 1, :]
        log_f = _log_sigmoid(gr_ref[0, 0, gi + 1] + gb[gi + 1:gi + 2, :])
        b = _scan_lanes(log_f, jnp.add, 0.0, reverse)
        a = log_i - b
        cm = _scan_lanes(a, jnp.maximum, -jnp.inf, reverse)
        per_dir.append((a, b))
        col_rows += [cm, b]
    pad = jnp.zeros((LANES - 4 * nc, C), F32)
    colv[...] = jnp.concatenate(col_rows + [pad], axis=0).T

    ti = lax.broadcasted_iota(jnp.int32, (C, C), 0)
    si = lax.broadcasted_iota(jnp.int32, (C, C), 1)
    for di in range(2):
        forward = di == 0
        a_all, b_all = per_dir[di]
        feeds = (si <= ti) if forward else (si >= ti)
        ct[...] = jnp.zeros_like(ct)
        m_prev = jnp.zeros((1, 1), F32)
        for step in range(nc):
            c = step if forward else nc - 1 - step
            rows = slice(c * C, (c + 1) * C)
            q = qs[rows, :]
            va = vaug[rows, :]
            a_row = a_all[c:c + 1, :]
            b_last = b_all[c:c + 1, C - 1:C] if forward else b_all[c:c + 1, 0:1]
            col = 2 * nc * di + c
            m_col = jnp.maximum(colv[:, col:col + 1], m_prev)
            b_col = colv[:, col + nc:col + nc + 1]
            e = jnp.exp(jnp.where(feeds, a_row - m_col, -jnp.inf))
            s = jnp.dot(q, ktb[:, rows], preferred_element_type=F32)
            sc = (s * e).astype(BF16)
            w_inter = jnp.exp(m_prev - m_col)
            numa = (jnp.dot(sc, va, preferred_element_type=F32)
                    + w_inter * jnp.dot(q, ct[...].astype(BF16), preferred_element_type=F32))
            den = numa[:, Dh:Dh + 1]
            inv = 1.0 / jnp.maximum(jnp.abs(den), jnp.exp(-(b_col + m_col)))
            h = numa[:, :Dh] * inv
            if forward:
                hs[rows, :] = h
            else:
                hs[rows, :] += h
            if step < nc - 1:
                g_r = b_last + a_row
                m_new = jnp.maximum(b_last + m_prev, jnp.max(g_r, axis=1, keepdims=True))
                decay = jnp.exp(b_last + m_prev - m_new)
                ktw = (ktf[:, rows] * jnp.exp(g_r - m_new)).astype(BF16)
                ct[...] = decay * ct[...] + jnp.dot(ktw, va, preferred_element_type=F32)
                m_prev = m_new

    o_ref[...] = (jax.nn.sigmoid(mo_ref[...].astype(F32)) * hs[...]).astype(o_ref.dtype)


def mlstm(proj, gates, conv_w, gate_b, *, batch, seq, col0):
    H, Dh, C = MLSTM_HEADS, MLSTM_HEAD_DIM, MLSTM_CHUNK
    T = batch * seq
    nc = seq // C
    assert 4 * nc <= LANES
    c0 = col0 // Dh
    g_row = gates[:, :4 * H].reshape(batch, seq, 4, H).transpose(0, 3, 2, 1)
    g_row = g_row.reshape(batch, H, 4, nc, C)
    gb = gate_b.reshape(4, H).T.reshape(H, 4, 1)
    cw_t = conv_w.T
    blk = lambda j: pl.BlockSpec((seq, Dh), lambda b, h: (b, c0 + j * H + h))
    return pl.pallas_call(
        _mlstm_kernel,
        grid=(batch, H),
        in_specs=[blk(0), blk(1), blk(2), blk(3),
                  pl.BlockSpec((Dh, CONV_WIDTH), lambda b, h: (h, 0)),
                  pl.BlockSpec((Dh, CONV_WIDTH), lambda b, h: (H + h, 0)),
                  pl.BlockSpec((1, 1, 4, nc, C), lambda b, h: (b, h, 0, 0, 0)),
                  pl.BlockSpec((1, 4, 1), lambda b, h: (h, 0, 0))],
        out_specs=pl.BlockSpec((seq, Dh), lambda b, h: (b, h)),
        out_shape=jax.ShapeDtypeStruct((T, H * Dh), BF16),
        scratch_shapes=[pltpu.VMEM((seq, Dh), BF16),
                        pltpu.VMEM((Dh, seq), F32),
                        pltpu.VMEM((Dh, seq), BF16),
                        pltpu.VMEM((seq, Dh + LANES), BF16),
                        pltpu.VMEM((seq, Dh), F32),
                        pltpu.VMEM((Dh, Dh + LANES), F32),
                        pltpu.VMEM((C, LANES), F32)],
        compiler_params=_params("parallel", "parallel"),
        name="bidir_mlstm",
    )(proj, proj, proj, proj, cw_t, cw_t, g_row, gb)


def _router_kernel(x_ref, wr_ref, br_ref, g_ref, rank_ref, cnt_ref):
    tm = x_ref.shape[0]

    @pl.when(pl.program_id(0) == 0)
    def _():
        cnt_ref[...] = jnp.zeros_like(cnt_ref)

    logits = jnp.dot(x_ref[...], wr_ref[...], preferred_element_type=F32,
                     precision=lax.Precision.HIGHEST) + br_ref[...]
    lane = lax.broadcasted_iota(jnp.int32, logits.shape, 1)
    lg = jnp.where(lane < N_EXPERTS, logits, -jnp.inf)
    v1 = jnp.max(lg, axis=1, keepdims=True)
    i1 = jnp.min(jnp.where(lg == v1, lane, LANES), axis=1, keepdims=True)
    lg2 = jnp.where(lane == i1, -jnp.inf, lg)
    v2 = jnp.max(lg2, axis=1, keepdims=True)
    i2 = jnp.min(jnp.where(lg2 == v2, lane, LANES), axis=1, keepdims=True)
    e = jnp.exp(v2 - v1)
    p1 = 1.0 / (1.0 + e)
    p2 = e / (1.0 + e)
    g_ref[...] = jnp.where(lane == i1, p1, 0.0) + jnp.where(lane == i2, p2, 0.0)
    sel = (lane == i1) | (lane == i2)
    self_ = jnp.where(sel, 1.0, 0.0)
    row = lax.broadcasted_iota(jnp.int32, (tm, tm), 0)
    col = lax.broadcasted_iota(jnp.int32, (tm, tm), 1)
    earlier = jnp.where(col < row, 1.0, 0.0).astype(BF16)
    raw = (jnp.dot(earlier, self_.astype(BF16), preferred_element_type=F32)
           + cnt_ref[...]).astype(jnp.int32)
    rank_ref[...] = jnp.where(sel, raw, -1 - raw)
    cnt_ref[...] += jnp.sum(self_, axis=0, keepdims=True)


def router(x, wr, br, *, tm):
    T, D = x.shape
    wr_p = jnp.zeros((D, LANES), F32).at[:, :N_EXPERTS].set(wr)
    br_p = jnp.zeros((1, LANES), F32).at[0, :N_EXPERTS].set(br)
    return pl.pallas_call(
        _router_kernel,
        grid=(T // tm,),
        in_specs=[pl.BlockSpec((tm, D), lambda i: (i, 0)),
                  pl.BlockSpec((D, LANES), lambda i: (0, 0)),
                  pl.BlockSpec((1, LANES), lambda i: (0, 0))],
        out_specs=[pl.BlockSpec((tm, LANES), lambda i: (i, 0)),
                   pl.BlockSpec((tm, LANES), lambda i: (i, 0))],
        out_shape=[jax.ShapeDtypeStruct((T, LANES), F32),
                   jax.ShapeDtypeStruct((T, LANES), jnp.int32)],
        scratch_shapes=[pltpu.VMEM((1, LANES), F32)],
        compiler_params=_params("arbitrary"),
        name="moe_router",
    )(x, wr_p, br_p)


MOE_TILE = 256
MOE_TOKEN_BLOCK = 512
MOE_ROW_ALIGN = 16
MOE_STAGE_ROWS = 2 * MOE_TOKEN_BLOCK + N_EXPERTS * MOE_ROW_ALIGN


def _round_up(x, m):
    return (x + m - 1) // m * m


def _moe_plan(rank, gates, *, tile, tb):
    T = rank.shape[0]
    E, A = N_EXPERTS, MOE_ROW_ALIGN
    nb = T // tb
    r = rank[:, :E]
    sel = r >= 0
    raw = jnp.where(sel, r, -1 - r)
    cb = raw[::tb]
    counts = raw[-1] + sel[-1].astype(jnp.int32)
    cnt = jnp.concatenate([cb[1:], counts[None, :]], axis=0) - cb
    padn = _round_up(cnt, A)
    size = jnp.sum(padn, axis=0)
    gsize = _round_up(size, tile)
    ends = jnp.cumsum(gsize)
    start = ends - gsize
    lo = start[None, :] + jnp.cumsum(padn, axis=0) - padn
    off = jnp.cumsum(padn, axis=1) - padn
    within = raw - jnp.repeat(cb, tb, axis=0)
    local = jnp.repeat(off, tb, axis=0) + within
    slot_e = jnp.stack([jnp.argmax(sel, axis=1), E - 1 - jnp.argmax(sel[:, ::-1], axis=1)], axis=1)
    slot_row = jnp.take_along_axis(local, slot_e, axis=1).astype(jnp.int32)
    slot_gate = jnp.take_along_axis(gates[:, :E], slot_e, axis=1)
    n_tiles = -(-(2 * T + nb * E * (A - 1) + E * (tile - 1)) // tile)
    tile_starts = jnp.arange(n_tiles, dtype=jnp.int32) * tile
    count_le = lambda keys, x: jnp.sum((keys[None, :] <= x[:, None]).astype(jnp.int32), axis=1)
    tile_expert = jnp.clip(count_le(ends, tile_starts), 0, E - 1).astype(jnp.int32)
    tile_valid = (tile_starts < ends[-1]).astype(jnp.int32)
    flat = lambda a: a.reshape(-1).astype(jnp.int32)
    ranges = (flat(lo), flat(padn), flat(off))
    tails = ((start + size).astype(jnp.int32), (gsize - size).astype(jnp.int32),
             (ends[-1:] // tile).astype(jnp.int32))
    return slot_row, slot_gate, ranges, tails, tile_expert, tile_valid, n_tiles


def _for_each_piece(n, max_piece, fn):
    o = jnp.int32(0)
    size = max_piece
    while size >= MOE_ROW_ALIGN:
        take = (n & size) != 0

        @pl.when(take)
        def _(o=o, size=size):
            fn(pl.multiple_of(o, MOE_ROW_ALIGN), size)

        o = o + jnp.where(take, size, 0)
        size //= 2


def _gather_kernel(lo_ref, len_ref, off_ref, tail_lo_ref, tail_len_ref, used_tiles_ref,
                   x_ref, rows_ref, xs_hbm, stage, sem):
    b = pl.program_id(0)
    tb = x_ref.shape[0]
    rid = lax.broadcasted_iota(jnp.int32, (stage.shape[0], tb), 0)
    rows = rows_ref[...]
    onehot = jnp.where((rows[0:1, :] == rid) | (rows[1:2, :] == rid), 1.0, 0.0).astype(BF16)
    stage[...] = jnp.dot(onehot, x_ref[...], preferred_element_type=F32).astype(BF16)

    def range_copies(act):
        for e in range(N_EXPERTS):
            k = b * N_EXPERTS + e
            lo = pl.multiple_of(lo_ref[k], MOE_ROW_ALIGN)
            off = pl.multiple_of(off_ref[k], MOE_ROW_ALIGN)
            _for_each_piece(len_ref[k], tb, lambda o, size: act(pltpu.make_async_copy(
                stage.at[pl.ds(off + o, size)], xs_hbm.at[pl.ds(lo + o, size)], sem)))

    range_copies(lambda c: c.start())
    range_copies(lambda c: c.wait())

    @pl.when(b == pl.num_programs(0) - 1)
    def _():
        stage[0:MOE_TILE, :] = jnp.zeros((MOE_TILE, stage.shape[1]), BF16)

        def tail_copies(act):
            for e in range(N_EXPERTS):
                lo = pl.multiple_of(tail_lo_ref[e], MOE_ROW_ALIGN)
                _for_each_piece(tail_len_ref[e], MOE_TILE // 2, lambda o, size: act(
                    pltpu.make_async_copy(stage.at[pl.ds(0, size)],
                                          xs_hbm.at[pl.ds(lo + o, size)], sem)))

        tail_copies(lambda c: c.start())
        tail_copies(lambda c: c.wait())

        def zero_tile(t, carry):
            cp = pltpu.make_async_copy(
                stage.at[pl.ds(0, MOE_TILE)],
                xs_hbm.at[pl.ds(pl.multiple_of(t * MOE_TILE, MOE_TILE), MOE_TILE)], sem)
            cp.start()
            cp.wait()
            return carry

        lax.fori_loop(used_tiles_ref[0], xs_hbm.shape[0] // MOE_TILE, zero_tile, 0)


def moe_gather(xb, slot_row_t, ranges, tails, *, n_tiles, tile, tb):
    T, D = xb.shape
    grid_spec = pltpu.PrefetchScalarGridSpec(
        num_scalar_prefetch=6, grid=(T // tb,),
        in_specs=[pl.BlockSpec((tb, D), lambda b, *_: (b, 0)),
                  pl.BlockSpec((2, tb), lambda b, *_: (0, b))],
        out_specs=pl.BlockSpec(memory_space=pl.ANY),
        scratch_shapes=[pltpu.VMEM((MOE_STAGE_ROWS, D), BF16), pltpu.SemaphoreType.DMA(())])
    return pl.pallas_call(
        _gather_kernel, grid_spec=grid_spec,
        out_shape=jax.ShapeDtypeStruct((n_tiles * tile, D), BF16),
        compiler_params=_params("arbitrary"), name="moe_gather",
    )(*ranges, *tails, xb, slot_row_t)


def _expert_swiglu_kernel(te, tv, x_ref, w1_ref, w3_ref, o_ref):
    i = pl.program_id(1)

    @pl.when(tv[i] == 1)
    def _():
        a = x_ref[...]
        h1 = jnp.dot(a, w1_ref[0], preferred_element_type=F32)
        h3 = jnp.dot(a, w3_ref[0], preferred_element_type=F32)
        o_ref[...] = (h1 * jax.nn.sigmoid(h1) * h3).astype(o_ref.dtype)

    @pl.when(tv[i] == 0)
    def _():
        o_ref[...] = jnp.zeros_like(o_ref)


def expert_swiglu(xs, w1, w3, tile_expert, tile_valid, *, tile, tf):
    R, D = xs.shape
    E, _, F = w1.shape
    n_tiles = R // tile
    grid_spec = pltpu.PrefetchScalarGridSpec(
        num_scalar_prefetch=2, grid=(F // tf, n_tiles),
        in_specs=[pl.BlockSpec((tile, D), lambda f, i, te, tv: (i, 0)),
                  pl.BlockSpec((1, D, tf), lambda f, i, te, tv: (te[i], 0, f)),
                  pl.BlockSpec((1, D, tf), lambda f, i, te, tv: (te[i], 0, f))],
        out_specs=pl.BlockSpec((tile, tf), lambda f, i, te, tv: (i, f)))
    return pl.pallas_call(
        _expert_swiglu_kernel, grid_spec=grid_spec,
        out_shape=jax.ShapeDtypeStruct((R, F), BF16),
        compiler_params=_params("parallel", "parallel"), name="expert_swiglu",
    )(tile_expert, tile_valid, xs, w1, w3)


def _expert_w2_kernel(te, tv, h_ref, w2_ref, o_ref):
    i = pl.program_id(0)

    @pl.when(tv[i] == 1)
    def _():
        o_ref[...] = jnp.dot(h_ref[...], w2_ref[0], preferred_element_type=F32).astype(o_ref.dtype)

    @pl.when(tv[i] == 0)
    def _():
        o_ref[...] = jnp.zeros_like(o_ref)


def expert_w2(h, w2, tile_expert, tile_valid, *, tile):
    R, F = h.shape
    D = w2.shape[2]
    grid_spec = pltpu.PrefetchScalarGridSpec(
        num_scalar_prefetch=2, grid=(R // tile,),
        in_specs=[pl.BlockSpec((tile, F), lambda i, te, tv: (i, 0)),
                  pl.BlockSpec((1, F, D), lambda i, te, tv: (te[i], 0, 0))],
        out_specs=pl.BlockSpec((tile, D), lambda i, te, tv: (i, 0)))
    return pl.pallas_call(
        _expert_w2_kernel, grid_spec=grid_spec,
        out_shape=jax.ShapeDtypeStruct((R, D), BF16),
        compiler_params=_params("parallel"), name="expert_w2",
    )(tile_expert, tile_valid, h, w2)


def _combine_kernel(lo_ref, len_ref, off_ref, y_hbm, rows_ref, gate_ref, res_ref, lng_ref, lnb_ref,
                    o_ref, ob_ref, stage, sem, *, alpha):
    b = pl.program_id(0)
    tb = rows_ref.shape[0]

    @pl.when(b == 0)
    def _():
        stage[...] = jnp.zeros_like(stage)

    def range_copies(act):
        for e in range(N_EXPERTS):
            k = b * N_EXPERTS + e
            lo = pl.multiple_of(lo_ref[k], MOE_ROW_ALIGN)
            off = pl.multiple_of(off_ref[k], MOE_ROW_ALIGN)
            _for_each_piece(len_ref[k], tb, lambda o, size: act(pltpu.make_async_copy(
                y_hbm.at[pl.ds(lo + o, size)], stage.at[pl.ds(off + o, size)], sem)))

    range_copies(lambda c: c.start())
    range_copies(lambda c: c.wait())

    cid = lax.broadcasted_iota(jnp.int32, (tb, stage.shape[0]), 1)
    rows, gate = rows_ref[...], gate_ref[...]
    weights = (jnp.where(rows[:, 0:1] == cid, gate[:, 0:1], 0.0)
               + jnp.where(rows[:, 1:2] == cid, gate[:, 1:2], 0.0)).astype(BF16)
    f = jnp.dot(weights, stage[...], preferred_element_type=F32)
    out = _layer_norm_rows(alpha * res_ref[...] + f, lng_ref[...], lnb_ref[...])
    o_ref[...] = out
    ob_ref[...] = out.astype(BF16)


def moe_combine_ln(y, slot_row, slot_gate, res, g, beta, ranges, *, tb, alpha):
    T, D = res.shape
    tok = lambda b, *_: (b, 0)
    const = lambda b, *_: (0, 0)
    grid_spec = pltpu.PrefetchScalarGridSpec(
        num_scalar_prefetch=3, grid=(T // tb,),
        in_specs=[pl.BlockSpec(memory_space=pl.ANY),
                  pl.BlockSpec((tb, 2), tok), pl.BlockSpec((tb, 2), tok),
                  pl.BlockSpec((tb, D), tok),
                  pl.BlockSpec((1, D), const), pl.BlockSpec((1, D), const)],
        out_specs=[pl.BlockSpec((tb, D), tok), pl.BlockSpec((tb, D), tok)],
        scratch_shapes=[pltpu.VMEM((MOE_STAGE_ROWS, D), BF16), pltpu.SemaphoreType.DMA(())])
    return pl.pallas_call(
        functools.partial(_combine_kernel, alpha=alpha), grid_spec=grid_spec,
        out_shape=[jax.ShapeDtypeStruct((T, D), F32), jax.ShapeDtypeStruct((T, D), BF16)],
        compiler_params=_params("arbitrary"), name="moe_combine_ln",
    )(*ranges, y, slot_row, slot_gate, res, g.reshape(1, D), beta.reshape(1, D))


def moe_layer(xf, xb, wr, br, w1, w3, w2, ln_g, ln_b, *, alpha):
    tile, tb = MOE_TILE, MOE_TOKEN_BLOCK
    gates, rank = router(xf, wr, br, tm=1024)
    slot_row, slot_gate, ranges, tails, tile_expert, tile_valid, n_tiles = _moe_plan(
        rank, gates, tile=tile, tb=tb)
    xs = moe_gather(xb, slot_row.T, ranges, tails, n_tiles=n_tiles, tile=tile, tb=tb)
    h = expert_swiglu(xs, w1, w3, tile_expert, tile_valid, tile=tile, tf=1408)
    y = expert_w2(h, w2, tile_expert, tile_valid, tile=tile)
    return moe_combine_ln(y, slot_row, slot_gate, xf, ln_g, ln_b, ranges, tb=tb, alpha=alpha)


def kernel(x, w_in, w_out, conv_w, gate_b, rpb_table, ln_g, ln_b, dense_w1, dense_w3,
           dense_w2, router_w, router_b, moe_w1, moe_w3, moe_w2):
    B, S, D = x.shape
    T = B * S
    depth = w_in.shape[0]
    alpha = (2 * depth) ** 0.25
    attn_w = ATTN_HEADS * ATTN_HEAD_DIM
    mlstm_w = MLSTM_HEADS * MLSTM_HEAD_DIM
    main_cols = 3 * attn_w + 4 * mlstm_w

    bias_tiles = _attn_bias_tiles(rpb_table)
    xf = x.reshape(T, D)
    xb = xf.astype(BF16)
    for l in range(depth):
        w_gate = jnp.pad(w_in[l, :, main_cols:], ((0, 0), (0, LANES - 4 * MLSTM_HEADS))).astype(BF16)
        proj = in_projection(xb, w_in, l, n_cols=main_cols, tm=1024, tn=1024)
        gates = matmul(xb, w_gate, tm=2048, tn=LANES, out_dtype=F32, name="gate_proj")
        attn = attention(proj, bias_tiles, batch=B, seq=S)
        rec = mlstm(proj, gates, conv_w[l], gate_b[l], batch=B, seq=S, col0=3 * attn_w)
        wo = w_out[l].astype(BF16)
        xf, xb = matmul_residual_ln([attn, rec], [wo[:attn_w], wo[attn_w:]], xf,
                                    ln_g[l, 0], ln_b[l, 0], tm=512, sub=256, alpha=alpha,
                                    name="out_proj_ln")
        j = l // 2
        if l % 2 == 0:
            hmid = swiglu_matmul(xb, dense_w1[j].astype(BF16), dense_w3[j].astype(BF16),
                                 tm=1024, tf=1408, name="dense_swiglu")
            xf, xb = matmul_residual_ln([hmid], [dense_w2[j].astype(BF16)], xf,
                                        ln_g[l, 1], ln_b[l, 1], tm=256, sub=256, alpha=alpha,
                                        name="dense_w2_ln")
        else:
            xf, xb = moe_layer(xf, xb, router_w[j], router_b[j], moe_w1[j].astype(BF16),
                               moe_w3[j].astype(BF16), moe_w2[j].astype(BF16),
                               ln_g[l, 1], ln_b[l, 1], alpha=alpha)
    return xf.reshape(B, S, D)
```

```python
import functools
import math

import jax
import jax.numpy as jnp
from jax import lax
from jax.experimental import pallas as pl
from jax.experimental.pallas import tpu as pltpu

F32 = jnp.float32
BF16 = jnp.bfloat16

ATTN_HEADS = 8
ATTN_HEAD_DIM = 128
MLSTM_HEADS = 4
MLSTM_HEAD_DIM = 256
DILATED_BRANCHES = ((128, 1), (512, 4), (2048, 16))
BAND_HALF = 64
REL_BUCKETS = 32
REL_MAX_DIST = 1024
NEG_INF = -1e30
CONV_WIDTH = 5
N_EXPERTS = 8
LN_EPS = 1e-5

V7X_VMEM_BYTES = 64 * 1024 * 1024
VMEM_LIMIT_BYTES = V7X_VMEM_BYTES - 8 * 1024 * 1024
LANES = 128

ATTN_Q_BLOCK = 128
ATTN_BATCH = 8
MLSTM_CHUNK = 256


def _params(*semantics):
    return pltpu.CompilerParams(dimension_semantics=semantics,
                                vmem_limit_bytes=VMEM_LIMIT_BYTES)


def _mm_kernel(a_ref, b_ref, o_ref):
    o_ref[...] = jnp.dot(a_ref[...], b_ref[...],
                         preferred_element_type=F32).astype(o_ref.dtype)


def matmul(a, b, *, tm, tn, out_dtype, name):
    M, K = a.shape
    N = b.shape[1]
    assert M % tm == 0 and N % tn == 0
    return pl.pallas_call(
        _mm_kernel,
        grid=(M // tm, N // tn),
        in_specs=[pl.BlockSpec((tm, K), lambda i, j: (i, 0)),
                  pl.BlockSpec((K, tn), lambda i, j: (0, j))],
        out_specs=pl.BlockSpec((tm, tn), lambda i, j: (i, j)),
        out_shape=jax.ShapeDtypeStruct((M, N), out_dtype),
        compiler_params=_params("parallel", "parallel"),
        name=name,
    )(a, b)


def _in_proj_kernel(a_ref, w_ref, o_ref, wb_ref):
    @pl.when(pl.program_id(1) == 0)
    def _():
        wb_ref[...] = w_ref[0].astype(BF16)

    o_ref[...] = jnp.dot(a_ref[...], wb_ref[...],
                         preferred_element_type=F32).astype(o_ref.dtype)


def in_projection(a, w_stack, layer, *, n_cols, tm, tn):
    M, K = a.shape
    assert M % tm == 0 and n_cols % tn == 0
    return pl.pallas_call(
        _in_proj_kernel,
        grid=(n_cols // tn, M // tm),
        in_specs=[pl.BlockSpec((tm, K), lambda j, i: (i, 0)),
                  pl.BlockSpec((1, K, tn), lambda j, i: (layer, 0, j))],
        out_specs=pl.BlockSpec((tm, tn), lambda j, i: (i, j)),
        out_shape=jax.ShapeDtypeStruct((M, n_cols), BF16),
        scratch_shapes=[pltpu.VMEM((K, tn), BF16)],
        compiler_params=_params("parallel", "arbitrary"),
        name="in_proj",
    )(a, w_stack)


def _swiglu_kernel(a_ref, w1_ref, w3_ref, o_ref):
    a = a_ref[...]
    h1 = jnp.dot(a, w1_ref[...], preferred_element_type=F32)
    h3 = jnp.dot(a, w3_ref[...], preferred_element_type=F32)
    o_ref[...] = (h1 * jax.nn.sigmoid(h1) * h3).astype(o_ref.dtype)


def swiglu_matmul(a, w1, w3, *, tm, tf, name):
    M, K = a.shape
    F = w1.shape[1]
    assert M % tm == 0 and F % tf == 0
    return pl.pallas_call(
        _swiglu_kernel,
        grid=(M // tm, F // tf),
        in_specs=[pl.BlockSpec((tm, K), lambda i, j: (i, 0)),
                  pl.BlockSpec((K, tf), lambda i, j: (0, j)),
                  pl.BlockSpec((K, tf), lambda i, j: (0, j))],
        out_specs=pl.BlockSpec((tm, tf), lambda i, j: (i, j)),
        out_shape=jax.ShapeDtypeStruct((M, F), BF16),
        compiler_params=_params("parallel", "parallel"),
        name=name,
    )(a, w1, w3)


def _layer_norm_rows(z, g, b):
    mu = jnp.mean(z, axis=-1, keepdims=True)
    zc = z - mu
    var = jnp.mean(zc * zc, axis=-1, keepdims=True)
    return zc * lax.rsqrt(var + LN_EPS) * g + b


def _mm_res_ln_kernel(*refs, n_in, sub, alpha):
    a_refs, b_refs = refs[:n_in], refs[n_in:2 * n_in]
    res_ref, lng_ref, lnb_ref, o_ref, ob_ref = refs[2 * n_in:]
    for s0 in range(0, o_ref.shape[0], sub):
        rows = slice(s0, s0 + sub)
        y = None
        for a_ref, b_ref in zip(a_refs, b_refs):
            part = jnp.dot(a_ref[rows, :], b_ref[...], preferred_element_type=F32)
            y = part if y is None else y + part
        out = _layer_norm_rows(alpha * res_ref[rows, :] + y, lng_ref[...], lnb_ref[...])
        o_ref[rows, :] = out
        ob_ref[rows, :] = out.astype(BF16)


def matmul_residual_ln(a_list, b_list, res, g, beta, *, tm, sub, alpha, name):
    M, N = res.shape
    assert M % tm == 0 and tm % sub == 0
    row = lambda i: (i, 0)
    const = lambda i: (0, 0)
    in_specs = ([pl.BlockSpec((tm, a.shape[1]), row) for a in a_list]
                + [pl.BlockSpec(b.shape, const, pipeline_mode=pl.Buffered(1)) for b in b_list]
                + [pl.BlockSpec((tm, N), row), pl.BlockSpec((1, N), const),
                   pl.BlockSpec((1, N), const)])
    return pl.pallas_call(
        functools.partial(_mm_res_ln_kernel, n_in=len(a_list), sub=sub, alpha=alpha),
        grid=(M // tm,),
        in_specs=in_specs,
        out_specs=[pl.BlockSpec((tm, N), row), pl.BlockSpec((tm, N), row)],
        out_shape=[jax.ShapeDtypeStruct((M, N), F32), jax.ShapeDtypeStruct((M, N), BF16)],
        compiler_params=_params("parallel"),
        name=name,
    )(*a_list, *b_list, res, g.reshape(1, N), beta.reshape(1, N))


def _t5_bucket(rel):
    half = REL_BUCKETS // 2
    max_exact = half // 2
    n = jnp.abs(rel)
    large = max_exact + (jnp.log(jnp.maximum(n, 1).astype(F32) / max_exact)
                         / math.log(REL_MAX_DIST / max_exact)
                         * (half - max_exact)).astype(jnp.int32)
    large = jnp.minimum(large, half - 1)
    return jnp.where(rel > 0, half, 0) + jnp.where(n < max_exact, n, large)


_ATTN_WINDOW_DELTAS = (0, -BAND_HALF, -2 * BAND_HALF)


def _attn_bias_tiles(table):
    QB, W, N = ATTN_Q_BLOCK, 2 * ATTN_Q_BLOCK, 8 * ATTN_Q_BLOCK
    base = -min(_ATTN_WINDOW_DELTAS)
    j = jnp.arange(N)
    m = jnp.where(j < N // 2, j, j - N) - base
    tiles = []
    for window, dilation in DILATED_BRANCHES:
        assert window // (2 * dilation) == BAND_HALF
        vals = jnp.where((jnp.abs(m) <= BAND_HALF)[:, None],
                         table[_t5_bucket(m * dilation)].astype(F32), NEG_INF)
        flat = jnp.tile(vals.T, (1, QB))[:, :QB * (N - 1)]
        circ = flat.reshape(-1, QB, N - 1)
        tiles.append(jnp.stack([circ[:, :, base + d:base + d + W] for d in _ATTN_WINDOW_DELTAS]))
    return jnp.stack(tiles)


def _band_blocks(q, k, v, bias):
    s = jnp.einsum("nqd,nkd->nqk", q, k, preferred_element_type=F32)
    s = s * (ATTN_HEAD_DIM ** -0.5) + bias
    m = jnp.max(s, axis=-1, keepdims=True)
    p = jnp.exp(s - m)
    l = jnp.sum(p, axis=-1, keepdims=True)
    o = jnp.einsum("nqk,nkd->nqd", p.astype(BF16), v, preferred_element_type=F32) / l
    return o, m + jnp.log(l)


def _attn_kernel(q_ref, k_ref, v_ref, bias_ref, o_ref, qf, kf, vf, og, lg):
    S = q_ref.shape[0]
    QB = ATTN_Q_BLOCK
    qf[...] = q_ref[...].astype(F32)
    kf[...] = k_ref[...].astype(F32)
    vf[...] = v_ref[...].astype(F32)

    for g, (_, d) in enumerate(DILATED_BRANCHES):
        L = S // d
        assert L % QB == 0 and (L == QB or L >= 3 * QB)
        W = QB if L == QB else 2 * QB
        blocks = []
        for r in range(d):
            for n in range(L // QB):
                q0 = n * QB
                ks = 0 if L == QB else min(max(q0 - BAND_HALF, 0), L - 2 * QB)
                blocks.append((r + d * q0, r + d * ks, _ATTN_WINDOW_DELTAS.index(ks - q0)))

        def rows(ref32, ref16, start, n):
            if d == 1:
                return ref16[start:start + n, :]
            return ref32[pl.ds(start, n, stride=d), :].astype(BF16)

        for i0 in range(0, len(blocks), ATTN_BATCH):
            batch = blocks[i0:i0 + ATTN_BATCH]
            qb = jnp.stack([rows(qf, q_ref, qs, QB) for qs, _, _ in batch])
            kb = jnp.stack([rows(kf, k_ref, ks, W) for _, ks, _ in batch])
            vb = jnp.stack([rows(vf, v_ref, ks, W) for _, ks, _ in batch])
            bias = jnp.stack([bias_ref[g, case, 0][:, :W] for _, _, case in batch])
            o, lse = _band_blocks(qb, kb, vb, bias)
            for j, (qs, _, _) in enumerate(batch):
                idx = slice(qs, qs + QB) if d == 1 else pl.ds(qs, QB, stride=d)
                og[g, idx, :] = o[j]
                lg[g, idx, :] = jnp.broadcast_to(lse[j], (QB, LANES))

    l0, l1, l2 = lg[0], lg[1], lg[2]
    mx = jnp.maximum(jnp.maximum(l0, l1), l2)
    e0, e1, e2 = jnp.exp(l0 - mx), jnp.exp(l1 - mx), jnp.exp(l2 - mx)
    out = (e0 * og[0] + e1 * og[1] + e2 * og[2]) / (e0 + e1 + e2)
    o_ref[...] = out.astype(o_ref.dtype)


def attention(proj, bias_tiles, *, batch, seq):
    H, Dh = ATTN_HEADS, ATTN_HEAD_DIM
    assert Dh == LANES
    T = batch * seq
    blk = lambda off: pl.BlockSpec((seq, Dh), lambda h, b: (b, off + h))
    return pl.pallas_call(
        _attn_kernel,
        grid=(H, batch),
        in_specs=[blk(0), blk(H), blk(2 * H),
                  pl.BlockSpec((3, 3, 1, ATTN_Q_BLOCK, 2 * ATTN_Q_BLOCK),
                               lambda h, b: (0, 0, h, 0, 0))],
        out_specs=pl.BlockSpec((seq, Dh), lambda h, b: (b, h)),
        out_shape=jax.ShapeDtypeStruct((T, H * Dh), BF16),
        scratch_shapes=[pltpu.VMEM((seq, Dh), F32)] * 3
                       + [pltpu.VMEM((3, seq, Dh), F32)] * 2,
        compiler_params=_params("parallel", "parallel"),
        name="dilated_attention",
    )(proj, proj, proj, bias_tiles)


def _log_sigmoid(x):
    return jnp.minimum(x, 0.0) - jnp.log1p(jnp.exp(-jnp.abs(x)))


def _scan_lanes(x, combine, identity, reverse):
    n = x.shape[-1]
    axis = x.ndim - 1
    lane = lax.broadcasted_iota(jnp.int32, x.shape, axis)
    k = 1
    while k < n:
        if reverse:
            shifted = jnp.where(lane < n - k, pltpu.roll(x, n - k, axis=axis), identity)
        else:
            shifted = jnp.where(lane >= k, pltpu.roll(x, k, axis=axis), identity)
        x = combine(x, shifted)
        k *= 2
    return x


def _mlstm_kernel(mq_ref, mk_ref, mv_ref, mo_ref, cwq_ref, cwk_ref, gr_ref, gb_ref, o_ref,
                  qs, ktf, ktb, vaug, hs, ct, colv):
    S, Dh = mq_ref.shape
    C = MLSTM_CHUNK
    nc = S // C
    halo = CONV_WIDTH // 2
    lane_s = lax.broadcasted_iota(jnp.int32, (Dh, S), 1)

    def conv_silu_t(src_ref, wt_ref):
        xt = src_ref[...].astype(F32).T
        wt = wt_ref[...]
        acc = xt * wt[:, halo:halo + 1]
        for j in range(CONV_WIDTH):
            off = j - halo
            if off == 0:
                continue
            shifted = pltpu.roll(xt, (-off) % S, axis=1)
            ok = (lane_s < S - off) if off > 0 else (lane_s >= -off)
            acc = acc + jnp.where(ok, shifted, 0.0) * wt[:, j:j + 1]
        return acc * jax.nn.sigmoid(acc)

    qs[...] = conv_silu_t(mq_ref, cwq_ref).T.astype(BF16)
    kt = conv_silu_t(mk_ref, cwk_ref) * (Dh ** -0.5)
    ktf[...] = kt
    ktb[...] = kt.astype(BF16)
    vaug[:, :Dh] = mv_ref[...]
    vaug[:, Dh:] = jnp.ones((S, LANES), BF16)

    gb = gb_ref[0]
    per_dir, col_rows = [], []
    for di in range(2):
        reverse = di == 1
        gi = 2 * di
        log_i = gr_ref[0, 0, gi] + gb[gi:gi + 1, :]
        log_f = _log_sigmoid(gr_ref[0, 0, gi + 1] + gb[gi + 1:gi + 2, :])
        b = _scan_lanes(log_f, jnp.add, 0.0, reverse)
        a = log_i - b
        cm = _scan_lanes(a, jnp.maximum, -jnp.inf, reverse)
        per_dir.append((a, b))
        col_rows += [cm, b]
    pad = jnp.zeros((LANES - 4 * nc, C), F32)
    colv[...] = jnp.concatenate(col_rows + [pad], axis=0).T

    ti = lax.broadcasted_iota(jnp.int32, (C, C), 0)
    si = lax.broadcasted_iota(jnp.int32, (C, C), 1)
    for di in range(2):
        forward = di == 0
        a_all, b_all = per_dir[di]
        feeds = (si <= ti) if forward else (si >= ti)
        ct[...] = jnp.zeros_like(ct)
        m_prev = jnp.zeros((1, 1), F32)
        for step in range(nc):
            c = step if forward else nc - 1 - step
            rows = slice(c * C, (c + 1) * C)
            q = qs[rows, :]
            va = vaug[rows, :]
            a_row = a_all[c:c + 1, :]
            b_last = b_all[c:c + 1, C - 1:C] if forward else b_all[c:c + 1, 0:1]
            col = 2 * nc * di + c
            m_col = jnp.maximum(colv[:, col:col + 1], m_prev)
            b_col = colv[:, col + nc:col + nc + 1]
            e = jnp.exp(jnp.where(feeds, a_row - m_col, -jnp.inf))
            s = jnp.dot(q, ktb[:, rows], preferred_element_type=F32)
            sc = (s * e).astype(BF16)
            w_inter = jnp.exp(m_prev - m_col)
            numa = (jnp.dot(sc, va, preferred_element_type=F32)
                    + w_inter * jnp.dot(q, ct[...].astype(BF16), preferred_element_type=F32))
            den = numa[:, Dh:Dh + 1]
            inv = 1.0 / jnp.maximum(jnp.abs(den), jnp.exp(-(b_col + m_col)))
            h = numa[:, :Dh] * inv
            if forward:
                hs[rows, :] = h
            else:
                hs[rows, :] += h
            if step < nc - 1:
                g_r = b_last + a_row
                m_new = jnp.maximum(b_last + m_prev, jnp.max(g_r, axis=1, keepdims=True))
                decay = jnp.exp(b_last + m_prev - m_new)
                ktw = (ktf[:, rows] * jnp.exp(g_r - m_new)).astype(BF16)
                ct[...] = decay * ct[...] + jnp.dot(ktw, va, preferred_element_type=F32)
                m_prev = m_new

    o_ref[...] = (jax.nn.sigmoid(mo_ref[...].astype(F32)) * hs[...]).astype(o_ref.dtype)


def mlstm(proj, gates, conv_w, gate_b, *, batch, seq, col0):
    H, Dh, C = MLSTM_HEADS, MLSTM_HEAD_DIM, MLSTM_CHUNK
    T = batch * seq
    nc = seq // C
    assert 4 * nc <= LANES
    c0 = col0 // Dh
    g_row = gates[:, :4 * H].reshape(batch, seq, 4, H).transpose(0, 3, 2, 1)
    g_row = g_row.reshape(batch, H, 4, nc, C)
    gb = gate_b.reshape(4, H).T.reshape(H, 4, 1)
    cw_t = conv_w.T
    blk = lambda j: pl.BlockSpec((seq, Dh), lambda b, h: (b, c0 + j * H + h))
    return pl.pallas_call(
        _mlstm_kernel,
        grid=(batch, H),
        in_specs=[blk(0), blk(1), blk(2), blk(3),
                  pl.BlockSpec((Dh, CONV_WIDTH), lambda b, h: (h, 0)),
                  pl.BlockSpec((Dh, CONV_WIDTH), lambda b, h: (H + h, 0)),
                  pl.BlockSpec((1, 1, 4, nc, C), lambda b, h: (b, h, 0, 0, 0)),
                  pl.BlockSpec((1, 4, 1), lambda b, h: (h, 0, 0))],
        out_specs=pl.BlockSpec((seq, Dh), lambda b, h: (b, h)),
        out_shape=jax.ShapeDtypeStruct((T, H * Dh), BF16),
        scratch_shapes=[pltpu.VMEM((seq, Dh), BF16),
                        pltpu.VMEM((Dh, seq), F32),
                        pltpu.VMEM((Dh, seq), BF16),
                        pltpu.VMEM((seq, Dh + LANES), BF16),
                        pltpu.VMEM((seq, Dh), F32),
                        pltpu.VMEM((Dh, Dh + LANES), F32),
                        pltpu.VMEM((C, LANES), F32)],
        compiler_params=_params("parallel", "parallel"),
        name="bidir_mlstm",
    )(proj, proj, proj, proj, cw_t, cw_t, g_row, gb)


def _router_kernel(x_ref, wr_ref, br_ref, g_ref, rank_ref, cnt_ref):
    tm = x_ref.shape[0]

    @pl.when(pl.program_id(0) == 0)
    def _():
        cnt_ref[...] = jnp.zeros_like(cnt_ref)

    logits = jnp.dot(x_ref[...], wr_ref[...], preferred_element_type=F32,
                     precision=lax.Precision.HIGHEST) + br_ref[...]
    lane = lax.broadcasted_iota(jnp.int32, logits.shape, 1)
    lg = jnp.where(lane < N_EXPERTS, logits, -jnp.inf)
    v1 = jnp.max(lg, axis=1, keepdims=True)
    i1 = jnp.min(jnp.where(lg == v1, lane, LANES), axis=1, keepdims=True)
    lg2 = jnp.where(lane == i1, -jnp.inf, lg)
    v2 = jnp.max(lg2, axis=1, keepdims=True)
    i2 = jnp.min(jnp.where(lg2 == v2, lane, LANES), axis=1, keepdims=True)
    e = jnp.exp(v2 - v1)
    p1 = 1.0 / (1.0 + e)
    p2 = e / (1.0 + e)
    g_ref[...] = jnp.where(lane == i1, p1, 0.0) + jnp.where(lane == i2, p2, 0.0)
    sel = (lane == i1) | (lane == i2)
    self_ = jnp.where(sel, 1.0, 0.0)
    row = lax.broadcasted_iota(jnp.int32, (tm, tm), 0)
    col = lax.broadcasted_iota(jnp.int32, (tm, tm), 1)
    earlier = jnp.where(col < row, 1.0, 0.0).astype(BF16)
    raw = (jnp.dot(earlier, self_.astype(BF16), preferred_element_type=F32)
           + cnt_ref[...]).astype(jnp.int32)
    rank_ref[...] = jnp.where(sel, raw, -1 - raw)
    cnt_ref[...] += jnp.sum(self_, axis=0, keepdims=True)


def router(x, wr, br, *, tm):
    T, D = x.shape
    wr_p = jnp.zeros((D, LANES), F32).at[:, :N_EXPERTS].set(wr)
    br_p = jnp.zeros((1, LANES), F32).at[0, :N_EXPERTS].set(br)
    return pl.pallas_call(
        _router_kernel,
        grid=(T // tm,),
        in_specs=[pl.BlockSpec((tm, D), lambda i: (i, 0)),
                  pl.BlockSpec((D, LANES), lambda i: (0, 0)),
                  pl.BlockSpec((1, LANES), lambda i: (0, 0))],
        out_specs=[pl.BlockSpec((tm, LANES), lambda i: (i, 0)),
                   pl.BlockSpec((tm, LANES), lambda i: (i, 0))],
        out_shape=[jax.ShapeDtypeStruct((T, LANES), F32),
                   jax.ShapeDtypeStruct((T, LANES), jnp.int32)],
        scratch_shapes=[pltpu.VMEM((1, LANES), F32)],
        compiler_params=_params("arbitrary"),
        name="moe_router",
    )(x, wr_p, br_p)


MOE_TILE = 256
MOE_TOKEN_BLOCK = 512
MOE_ROW_ALIGN = 16
MOE_STAGE_ROWS = 2 * MOE_TOKEN_BLOCK + N_EXPERTS * MOE_ROW_ALIGN


def _round_up(x, m):
    return (x + m - 1) // m * m


def _moe_plan(rank, gates, *, tile, tb):
    T = rank.shape[0]
    E, A = N_EXPERTS, MOE_ROW_ALIGN
    nb = T // tb
    r = rank[:, :E]
    sel = r >= 0
    raw = jnp.where(sel, r, -1 - r)
    cb = raw[::tb]
    counts = raw[-1] + sel[-1].astype(jnp.int32)
    cnt = jnp.concatenate([cb[1:], counts[None, :]], axis=0) - cb
    padn = _round_up(cnt, A)
    size = jnp.sum(padn, axis=0)
    gsize = _round_up(size, tile)
    ends = jnp.cumsum(gsize)
    start = ends - gsize
    lo = start[None, :] + jnp.cumsum(padn, axis=0) - padn
    off = jnp.cumsum(padn, axis=1) - padn
    within = raw - jnp.repeat(cb, tb, axis=0)
    local = jnp.repeat(off, tb, axis=0) + within
    eid = jnp.arange(E, dtype=jnp.int32)[None, :]
    slot_e = jnp.stack([jnp.min(jnp.where(sel, eid, E - 1), axis=1),
                        jnp.max(jnp.where(sel, eid, 0), axis=1)], axis=1)
    slot_row = jnp.take_along_axis(local, slot_e, axis=1).astype(jnp.int32)
    slot_gate = jnp.take_along_axis(gates[:, :E], slot_e, axis=1)
    n_tiles = -(-(2 * T + nb * E * (A - 1) + E * (tile - 1)) // tile)
    tile_starts = jnp.arange(n_tiles, dtype=jnp.int32) * tile
    count_le = lambda keys, x: jnp.sum((keys[None, :] <= x[:, None]).astype(jnp.int32), axis=1)
    tile_expert = jnp.clip(count_le(ends, tile_starts), 0, E - 1).astype(jnp.int32)
    tile_valid = (tile_starts < ends[-1]).astype(jnp.int32)
    tile_new = jnp.concatenate([jnp.ones((1,), jnp.int32),
                                (tile_expert[1:] != tile_expert[:-1]).astype(jnp.int32)])
    flat = lambda a: a.reshape(-1).astype(jnp.int32)
    ranges = (flat(lo), flat(padn), flat(off))
    tails = ((start + size).astype(jnp.int32), (gsize - size).astype(jnp.int32),
             (ends[-1:] // tile).astype(jnp.int32))
    return slot_row, slot_gate, ranges, tails, (tile_expert, tile_valid, tile_new), n_tiles


def _for_each_piece(n, max_piece, fn):
    o = jnp.int32(0)
    size = max_piece
    while size >= MOE_ROW_ALIGN:
        take = (n & size) != 0

        @pl.when(take)
        def _(o=o, size=size):
            fn(pl.multiple_of(o, MOE_ROW_ALIGN), size)

        o = o + jnp.where(take, size, 0)
        size //= 2


def _gather_kernel(lo_ref, len_ref, off_ref, tail_lo_ref, tail_len_ref, used_tiles_ref,
                   x_ref, rows_ref, xs_hbm, stage, sems):
    b = pl.program_id(0)
    last = pl.num_programs(0) - 1
    slot = b % 2
    tb = x_ref.shape[0]

    def range_copies(blk, slot, act):
        for e in range(N_EXPERTS):
            k = blk * N_EXPERTS + e
            lo = pl.multiple_of(lo_ref[k], MOE_ROW_ALIGN)
            off = pl.multiple_of(off_ref[k], MOE_ROW_ALIGN)
            _for_each_piece(len_ref[k], tb, lambda o, size: act(pltpu.make_async_copy(
                stage.at[slot, pl.ds(off + o, size)], xs_hbm.at[pl.ds(lo + o, size)],
                sems.at[slot])))

    start = lambda c: c.start()
    wait = lambda c: c.wait()

    @pl.when(b >= 2)
    def _():
        range_copies(b - 2, slot, wait)

    rid = lax.broadcasted_iota(jnp.int32, (stage.shape[1], tb), 0)
    rows = rows_ref[...]
    onehot = jnp.where((rows[0:1, :] == rid) | (rows[1:2, :] == rid), 1.0, 0.0).astype(BF16)
    stage[slot] = jnp.dot(onehot, x_ref[...], preferred_element_type=F32).astype(BF16)
    range_copies(b, slot, start)

    @pl.when(b == last)
    def _():
        @pl.when(b >= 1)
        def _():
            range_copies(b - 1, 1 - slot, wait)

        range_copies(b, slot, wait)

        stage[slot, 0:MOE_TILE, :] = jnp.zeros((MOE_TILE, stage.shape[2]), BF16)

        def tail_copies(act):
            for e in range(N_EXPERTS):
                lo = pl.multiple_of(tail_lo_ref[e], MOE_ROW_ALIGN)
                _for_each_piece(tail_len_ref[e], MOE_TILE // 2, lambda o, size: act(
                    pltpu.make_async_copy(stage.at[slot, pl.ds(0, size)],
                                          xs_hbm.at[pl.ds(lo + o, size)], sems.at[slot])))

        tail_copies(start)
        tail_copies(wait)

        def zero_tile(t, carry):
            cp = pltpu.make_async_copy(
                stage.at[slot, pl.ds(0, MOE_TILE)],
                xs_hbm.at[pl.ds(pl.multiple_of(t * MOE_TILE, MOE_TILE), MOE_TILE)], sems.at[slot])
            cp.start()
            cp.wait()
            return carry

        lax.fori_loop(used_tiles_ref[0], xs_hbm.shape[0] // MOE_TILE, zero_tile, 0)


def moe_gather(xb, slot_row_t, ranges, tails, *, n_tiles, tile, tb):
    T, D = xb.shape
    grid_spec = pltpu.PrefetchScalarGridSpec(
        num_scalar_prefetch=6, grid=(T // tb,),
        in_specs=[pl.BlockSpec((tb, D), lambda b, *_: (b, 0)),
                  pl.BlockSpec((2, tb), lambda b, *_: (0, b))],
        out_specs=pl.BlockSpec(memory_space=pl.ANY),
        scratch_shapes=[pltpu.VMEM((2, MOE_STAGE_ROWS, D), BF16), pltpu.SemaphoreType.DMA((2,))])
    return pl.pallas_call(
        _gather_kernel, grid_spec=grid_spec,
        out_shape=jax.ShapeDtypeStruct((n_tiles * tile, D), BF16),
        compiler_params=_params("arbitrary"), name="moe_gather",
    )(*ranges, *tails, xb, slot_row_t)


def _expert_swiglu_kernel(te, tv, tnew, x_ref, w1_ref, w3_ref, o_ref, w1b, w3b):
    i = pl.program_id(1)

    @pl.when(tnew[i] == 1)
    def _():
        w1b[...] = w1_ref[0, 0].astype(BF16)
        w3b[...] = w3_ref[0, 0].astype(BF16)

    @pl.when(tv[i] == 1)
    def _():
        a = x_ref[...]
        h1 = jnp.dot(a, w1b[...], preferred_element_type=F32)
        h3 = jnp.dot(a, w3b[...], preferred_element_type=F32)
        o_ref[...] = (h1 * jax.nn.sigmoid(h1) * h3).astype(o_ref.dtype)

    @pl.when(tv[i] == 0)
    def _():
        o_ref[...] = jnp.zeros_like(o_ref)


def expert_swiglu(xs, w1, w3, layer, tile_expert, tile_valid, tile_new, *, tile, tf):
    R, D = xs.shape
    F = w1.shape[3]
    n_tiles = R // tile
    wspec = pl.BlockSpec((1, 1, D, tf), lambda f, i, te, tv, tn: (layer, te[i], 0, f),
                         pipeline_mode=pl.Buffered(1))
    grid_spec = pltpu.PrefetchScalarGridSpec(
        num_scalar_prefetch=3, grid=(F // tf, n_tiles),
        in_specs=[pl.BlockSpec((tile, D), lambda f, i, te, tv, tn: (i, 0)), wspec, wspec],
        out_specs=pl.BlockSpec((tile, tf), lambda f, i, te, tv, tn: (i, f)),
        scratch_shapes=[pltpu.VMEM((D, tf), BF16), pltpu.VMEM((D, tf), BF16)])
    return pl.pallas_call(
        _expert_swiglu_kernel, grid_spec=grid_spec,
        out_shape=jax.ShapeDtypeStruct((R, F), BF16),
        compiler_params=_params("parallel", "arbitrary"), name="expert_swiglu",
    )(tile_expert, tile_valid, tile_new, xs, w1, w3)


def _expert_w2_kernel(te, tv, tnew, h_ref, w2_ref, o_ref, w2b):
    i = pl.program_id(0)

    @pl.when(tnew[i] == 1)
    def _():
        w2b[...] = w2_ref[0, 0].astype(BF16)

    @pl.when(tv[i] == 1)
    def _():
        o_ref[...] = jnp.dot(h_ref[...], w2b[...], preferred_element_type=F32).astype(o_ref.dtype)

    @pl.when(tv[i] == 0)
    def _():
        o_ref[...] = jnp.zeros_like(o_ref)


def expert_w2(h, w2, layer, tile_expert, tile_valid, tile_new, *, tile):
    R, F = h.shape
    D = w2.shape[3]
    grid_spec = pltpu.PrefetchScalarGridSpec(
        num_scalar_prefetch=3, grid=(R // tile,),
        in_specs=[pl.BlockSpec((tile, F), lambda i, te, tv, tn: (i, 0)),
                  pl.BlockSpec((1, 1, F, D), lambda i, te, tv, tn: (layer, te[i], 0, 0),
                               pipeline_mode=pl.Buffered(1))],
        out_specs=pl.BlockSpec((tile, D), lambda i, te, tv, tn: (i, 0)),
        scratch_shapes=[pltpu.VMEM((F, D), BF16)])
    return pl.pallas_call(
        _expert_w2_kernel, grid_spec=grid_spec,
        out_shape=jax.ShapeDtypeStruct((R, D), BF16),
        compiler_params=_params("arbitrary"), name="expert_w2",
    )(tile_expert, tile_valid, tile_new, h, w2)


def _combine_kernel(lo_ref, len_ref, off_ref, y_hbm, rows_ref, gate_ref, res_ref, lng_ref, lnb_ref,
                    o_ref, ob_ref, stage, sems, *, alpha):
    b = pl.program_id(0)
    slot = b % 2
    tb = rows_ref.shape[0]

    def range_copies(blk, slot, act):
        for e in range(N_EXPERTS):
            k = blk * N_EXPERTS + e
            lo = pl.multiple_of(lo_ref[k], MOE_ROW_ALIGN)
            off = pl.multiple_of(off_ref[k], MOE_ROW_ALIGN)
            _for_each_piece(len_ref[k], tb, lambda o, size: act(pltpu.make_async_copy(
                y_hbm.at[pl.ds(lo + o, size)], stage.at[slot, pl.ds(off + o, size)],
                sems.at[slot])))

    @pl.when(b == 0)
    def _():
        stage[...] = jnp.zeros_like(stage)
        range_copies(0, 0, lambda c: c.start())

    @pl.when(b + 1 < pl.num_programs(0))
    def _():
        range_copies(b + 1, 1 - slot, lambda c: c.start())

    range_copies(b, slot, lambda c: c.wait())

    cid = lax.broadcasted_iota(jnp.int32, (tb, stage.shape[1]), 1)
    rows, gate = rows_ref[...], gate_ref[...]
    weights = (jnp.where(rows[:, 0:1] == cid, gate[:, 0:1], 0.0)
               + jnp.where(rows[:, 1:2] == cid, gate[:, 1:2], 0.0)).astype(BF16)
    f = jnp.dot(weights, stage[slot], preferred_element_type=F32)
    out = _layer_norm_rows(alpha * res_ref[...] + f, lng_ref[...], lnb_ref[...])
    o_ref[...] = out
    ob_ref[...] = out.astype(BF16)


def moe_combine_ln(y, slot_row, slot_gate, res, g, beta, ranges, *, tb, alpha):
    T, D = res.shape
    tok = lambda b, *_: (b, 0)
    const = lambda b, *_: (0, 0)
    grid_spec = pltpu.PrefetchScalarGridSpec(
        num_scalar_prefetch=3, grid=(T // tb,),
        in_specs=[pl.BlockSpec(memory_space=pl.ANY),
                  pl.BlockSpec((tb, 2), tok), pl.BlockSpec((tb, 2), tok),
                  pl.BlockSpec((tb, D), tok),
                  pl.BlockSpec((1, D), const), pl.BlockSpec((1, D), const)],
        out_specs=[pl.BlockSpec((tb, D), tok), pl.BlockSpec((tb, D), tok)],
        scratch_shapes=[pltpu.VMEM((2, MOE_STAGE_ROWS, D), BF16), pltpu.SemaphoreType.DMA((2,))])
    return pl.pallas_call(
        functools.partial(_combine_kernel, alpha=alpha), grid_spec=grid_spec,
        out_shape=[jax.ShapeDtypeStruct((T, D), F32), jax.ShapeDtypeStruct((T, D), BF16)],
        compiler_params=_params("arbitrary"), name="moe_combine_ln",
    )(*ranges, y, slot_row, slot_gate, res, g.reshape(1, D), beta.reshape(1, D))


def moe_layer(xf, xb, wr, br, w1, w3, w2, layer, ln_g, ln_b, *, alpha):
    tile, tb = MOE_TILE, MOE_TOKEN_BLOCK
    gates, rank = router(xf, wr, br, tm=1024)
    slot_row, slot_gate, ranges, tails, tiles, n_tiles = _moe_plan(rank, gates, tile=tile, tb=tb)
    xs = moe_gather(xb, slot_row.T, ranges, tails, n_tiles=n_tiles, tile=tile, tb=tb)
    h = expert_swiglu(xs, w1, w3, layer, *tiles, tile=tile, tf=1408)
    y = expert_w2(h, w2, layer, *tiles, tile=tile)
    return moe_combine_ln(y, slot_row, slot_gate, xf, ln_g, ln_b, ranges, tb=tb, alpha=alpha)


def kernel(x, w_in, w_out, conv_w, gate_b, rpb_table, ln_g, ln_b, dense_w1, dense_w3,
           dense_w2, router_w, router_b, moe_w1, moe_w3, moe_w2):
    B, S, D = x.shape
    T = B * S
    depth = w_in.shape[0]
    alpha = (2 * depth) ** 0.25
    attn_w = ATTN_HEADS * ATTN_HEAD_DIM
    mlstm_w = MLSTM_HEADS * MLSTM_HEAD_DIM
    main_cols = 3 * attn_w + 4 * mlstm_w

    bias_tiles = _attn_bias_tiles(rpb_table)
    xf = x.reshape(T, D)
    xb = xf.astype(BF16)
    w_gate = jnp.pad(lax.slice_in_dim(w_in, main_cols, w_in.shape[2], axis=2),
                     ((0, 0), (0, 0), (0, LANES - 4 * MLSTM_HEADS))).astype(BF16)
    for l in range(depth):
        proj = in_projection(xb, w_in, l, n_cols=main_cols, tm=1024, tn=1024)
        gates = matmul(xb, w_gate[l], tm=2048, tn=LANES, out_dtype=F32, name="gate_proj")
        attn = attention(proj, bias_tiles, batch=B, seq=S)
        rec = mlstm(proj, gates, conv_w[l], gate_b[l], batch=B, seq=S, col0=3 * attn_w)
        wo = w_out[l].astype(BF16)
        xf, xb = matmul_residual_ln([attn, rec], [wo[:attn_w], wo[attn_w:]], xf,
                                    ln_g[l, 0], ln_b[l, 0], tm=512, sub=256, alpha=alpha,
                                    name="out_proj_ln")
        j = l // 2
        if l % 2 == 0:
            hmid = swiglu_matmul(xb, dense_w1[j].astype(BF16), dense_w3[j].astype(BF16),
                                 tm=1024, tf=1408, name="dense_swiglu")
            xf, xb = matmul_residual_ln([hmid], [dense_w2[j].astype(BF16)], xf,
                                        ln_g[l, 1], ln_b[l, 1], tm=256, sub=256, alpha=alpha,
                                        name="dense_w2_ln")
        else:
            xf, xb = moe_layer(xf, xb, router_w[j], router_b[j], moe_w1, moe_w3, moe_w2, j,
                               ln_g[l, 1], ln_b[l, 1], alpha=alpha)
    return xf.reshape(B, S, D)
```

```python
import functools
import math

import jax
import jax.numpy as jnp
from jax import lax
from jax.experimental import pallas as pl
from jax.experimental.pallas import tpu as pltpu

F32 = jnp.float32
BF16 = jnp.bfloat16

ATTN_HEADS = 8
ATTN_HEAD_DIM = 128
MLSTM_HEADS = 4
MLSTM_HEAD_DIM = 256
DILATED_BRANCHES = ((128, 1), (512, 4), (2048, 16))
BAND_HALF = 64
REL_BUCKETS = 32
REL_MAX_DIST = 1024
NEG_INF = -1e30
CONV_WIDTH = 5
N_EXPERTS = 8
LN_EPS = 1e-5

V7X_VMEM_BYTES = 64 * 1024 * 1024
VMEM_LIMIT_BYTES = V7X_VMEM_BYTES - 8 * 1024 * 1024
LANES = 128

ATTN_Q_BLOCK = 128
ATTN_BATCH = 8
MLSTM_CHUNK = 256


def _params(*semantics):
    return pltpu.CompilerParams(dimension_semantics=semantics,
                                vmem_limit_bytes=VMEM_LIMIT_BYTES)


def _mm_kernel(a_ref, b_ref, o_ref):
    o_ref[...] = jnp.dot(a_ref[...], b_ref[...],
                         preferred_element_type=F32).astype(o_ref.dtype)


def matmul(a, b, *, tm, tn, out_dtype, name):
    M, K = a.shape
    N = b.shape[1]
    assert M % tm == 0 and N % tn == 0
    return pl.pallas_call(
        _mm_kernel,
        grid=(M // tm, N // tn),
        in_specs=[pl.BlockSpec((tm, K), lambda i, j: (i, 0)),
                  pl.BlockSpec((K, tn), lambda i, j: (0, j))],
        out_specs=pl.BlockSpec((tm, tn), lambda i, j: (i, j)),
        out_shape=jax.ShapeDtypeStruct((M, N), out_dtype),
        compiler_params=_params("parallel", "parallel"),
        name=name,
    )(a, b)


def _in_proj_kernel(a_ref, w_ref, o_ref, wb_ref):
    @pl.when(pl.program_id(1) == 0)
    def _():
        wb_ref[...] = w_ref[0].T.astype(BF16)

    o_ref[...] = jnp.dot(a_ref[...], wb_ref[...],
                         preferred_element_type=F32).astype(o_ref.dtype)


def in_projection(a, w_stack_t, layer, *, n_cols, tm, tn):
    M, K = a.shape
    assert M % tm == 0 and n_cols % tn == 0
    return pl.pallas_call(
        _in_proj_kernel,
        grid=(n_cols // tn, M // tm),
        in_specs=[pl.BlockSpec((tm, K), lambda j, i: (i, 0)),
                  pl.BlockSpec((1, tn, K), lambda j, i: (layer, j, 0))],
        out_specs=pl.BlockSpec((tm, tn), lambda j, i: (i, j)),
        out_shape=jax.ShapeDtypeStruct((M, n_cols), BF16),
        scratch_shapes=[pltpu.VMEM((K, tn), BF16)],
        compiler_params=_params("parallel", "arbitrary"),
        name="in_proj",
    )(a, w_stack_t)


def _swiglu_kernel(a_ref, w1_ref, w3_ref, o_ref):
    a = a_ref[...]
    h1 = jnp.dot(a, w1_ref[...], preferred_element_type=F32)
    h3 = jnp.dot(a, w3_ref[...], preferred_element_type=F32)
    o_ref[...] = (h1 * jax.nn.sigmoid(h1) * h3).astype(o_ref.dtype)


def swiglu_matmul(a, w1, w3, *, tm, tf, name):
    M, K = a.shape
    F = w1.shape[1]
    assert M % tm == 0 and F % tf == 0
    return pl.pallas_call(
        _swiglu_kernel,
        grid=(M // tm, F // tf),
        in_specs=[pl.BlockSpec((tm, K), lambda i, j: (i, 0)),
                  pl.BlockSpec((K, tf), lambda i, j: (0, j)),
                  pl.BlockSpec((K, tf), lambda i, j: (0, j))],
        out_specs=pl.BlockSpec((tm, tf), lambda i, j: (i, j)),
        out_shape=jax.ShapeDtypeStruct((M, F), BF16),
        compiler_params=_params("parallel", "parallel"),
        name=name,
    )(a, w1, w3)


def _layer_norm_rows(z, g, b):
    mu = jnp.mean(z, axis=-1, keepdims=True)
    zc = z - mu
    var = jnp.mean(zc * zc, axis=-1, keepdims=True)
    return zc * lax.rsqrt(var + LN_EPS) * g + b


def _mm_res_ln_kernel(*refs, n_in, sub, alpha):
    a_refs, b_refs = refs[:n_in], refs[n_in:2 * n_in]
    res_ref, lng_ref, lnb_ref, o_ref, ob_ref = refs[2 * n_in:]
    for s0 in range(0, o_ref.shape[0], sub):
        rows = slice(s0, s0 + sub)
        y = None
        for a_ref, b_ref in zip(a_refs, b_refs):
            part = jnp.dot(a_ref[rows, :], b_ref[...], preferred_element_type=F32)
            y = part if y is None else y + part
        out = _layer_norm_rows(alpha * res_ref[rows, :] + y, lng_ref[...], lnb_ref[...])
        o_ref[rows, :] = out
        ob_ref[rows, :] = out.astype(BF16)


def matmul_residual_ln(a_list, b_list, res, g, beta, *, tm, sub, alpha, name):
    M, N = res.shape
    assert M % tm == 0 and tm % sub == 0
    row = lambda i: (i, 0)
    const = lambda i: (0, 0)
    in_specs = ([pl.BlockSpec((tm, a.shape[1]), row) for a in a_list]
                + [pl.BlockSpec(b.shape, const, pipeline_mode=pl.Buffered(1)) for b in b_list]
                + [pl.BlockSpec((tm, N), row), pl.BlockSpec((1, N), const),
                   pl.BlockSpec((1, N), const)])
    return pl.pallas_call(
        functools.partial(_mm_res_ln_kernel, n_in=len(a_list), sub=sub, alpha=alpha),
        grid=(M // tm,),
        in_specs=in_specs,
        out_specs=[pl.BlockSpec((tm, N), row), pl.BlockSpec((tm, N), row)],
        out_shape=[jax.ShapeDtypeStruct((M, N), F32), jax.ShapeDtypeStruct((M, N), BF16)],
        compiler_params=_params("parallel"),
        name=name,
    )(*a_list, *b_list, res, g.reshape(1, N), beta.reshape(1, N))


def _t5_bucket(rel):
    half = REL_BUCKETS // 2
    max_exact = half // 2
    n = jnp.abs(rel)
    large = max_exact + (jnp.log(jnp.maximum(n, 1).astype(F32) / max_exact)
                         / math.log(REL_MAX_DIST / max_exact)
                         * (half - max_exact)).astype(jnp.int32)
    large = jnp.minimum(large, half - 1)
    return jnp.where(rel > 0, half, 0) + jnp.where(n < max_exact, n, large)


_ATTN_WINDOW_DELTAS = (0, -BAND_HALF, -2 * BAND_HALF)


def _attn_bias_tiles(table):
    QB, W, N = ATTN_Q_BLOCK, 2 * ATTN_Q_BLOCK, 8 * ATTN_Q_BLOCK
    base = -min(_ATTN_WINDOW_DELTAS)
    j = jnp.arange(N)
    m = jnp.where(j < N // 2, j, j - N) - base
    tiles = []
    for window, dilation in DILATED_BRANCHES:
        assert window // (2 * dilation) == BAND_HALF
        vals = jnp.where((jnp.abs(m) <= BAND_HALF)[:, None],
                         table[_t5_bucket(m * dilation)].astype(F32), NEG_INF)
        flat = jnp.tile(vals.T, (1, QB))[:, :QB * (N - 1)]
        circ = flat.reshape(-1, QB, N - 1)
        tiles.append(jnp.stack([circ[:, :, base + d:base + d + W] for d in _ATTN_WINDOW_DELTAS]))
    return jnp.stack(tiles)


def _band_blocks(q, k, v, bias):
    s = jnp.einsum("nqd,nkd->nqk", q, k, preferred_element_type=F32)
    s = s * (ATTN_HEAD_DIM ** -0.5) + bias
    m = jnp.max(s, axis=-1, keepdims=True)
    p = jnp.exp(s - m)
    l = jnp.sum(p, axis=-1, keepdims=True)
    o = jnp.einsum("nqk,nkd->nqd", p.astype(BF16), v, preferred_element_type=F32) / l
    return o, m + jnp.log(l)


def _attn_kernel(q_ref, k_ref, v_ref, bias_ref, o_ref, qf, kf, vf, og, lg):
    S = q_ref.shape[0]
    QB = ATTN_Q_BLOCK
    qf[...] = q_ref[...].astype(F32)
    kf[...] = k_ref[...].astype(F32)
    vf[...] = v_ref[...].astype(F32)

    for g, (_, d) in enumerate(DILATED_BRANCHES):
        L = S // d
        assert L % QB == 0 and (L == QB or L >= 3 * QB)
        W = QB if L == QB else 2 * QB
        blocks = []
        for r in range(d):
            for n in range(L // QB):
                q0 = n * QB
                ks = 0 if L == QB else min(max(q0 - BAND_HALF, 0), L - 2 * QB)
                blocks.append((r + d * q0, r + d * ks, _ATTN_WINDOW_DELTAS.index(ks - q0)))

        def rows(ref32, ref16, start, n):
            if d == 1:
                return ref16[start:start + n, :]
            return ref32[pl.ds(start, n, stride=d), :].astype(BF16)

        for i0 in range(0, len(blocks), ATTN_BATCH):
            batch = blocks[i0:i0 + ATTN_BATCH]
            qb = jnp.stack([rows(qf, q_ref, qs, QB) for qs, _, _ in batch])
            kb = jnp.stack([rows(kf, k_ref, ks, W) for _, ks, _ in batch])
            vb = jnp.stack([rows(vf, v_ref, ks, W) for _, ks, _ in batch])
            bias = jnp.stack([bias_ref[g, case, 0][:, :W] for _, _, case in batch])
            o, lse = _band_blocks(qb, kb, vb, bias)
            for j, (qs, _, _) in enumerate(batch):
                idx = slice(qs, qs + QB) if d == 1 else pl.ds(qs, QB, stride=d)
                og[g, idx, :] = o[j]
                lg[g, idx, :] = jnp.broadcast_to(lse[j], (QB, LANES))

    l0, l1, l2 = lg[0], lg[1], lg[2]
    mx = jnp.maximum(jnp.maximum(l0, l1), l2)
    e0, e1, e2 = jnp.exp(l0 - mx), jnp.exp(l1 - mx), jnp.exp(l2 - mx)
    out = (e0 * og[0] + e1 * og[1] + e2 * og[2]) / (e0 + e1 + e2)
    o_ref[...] = out.astype(o_ref.dtype)


def attention(proj, bias_tiles, *, batch, seq):
    H, Dh = ATTN_HEADS, ATTN_HEAD_DIM
    assert Dh == LANES
    T = batch * seq
    blk = lambda off: pl.BlockSpec((seq, Dh), lambda h, b: (b, off + h))
    return pl.pallas_call(
        _attn_kernel,
        grid=(H, batch),
        in_specs=[blk(0), blk(H), blk(2 * H),
                  pl.BlockSpec((3, 3, 1, ATTN_Q_BLOCK, 2 * ATTN_Q_BLOCK),
                               lambda h, b: (0, 0, h, 0, 0))],
        out_specs=pl.BlockSpec((seq, Dh), lambda h, b: (b, h)),
        out_shape=jax.ShapeDtypeStruct((T, H * Dh), BF16),
        scratch_shapes=[pltpu.VMEM((seq, Dh), F32)] * 3
                       + [pltpu.VMEM((3, seq, Dh), F32)] * 2,
        compiler_params=_params("parallel", "parallel"),
        name="dilated_attention",
    )(proj, proj, proj, bias_tiles)


def _log_sigmoid(x):
    return jnp.minimum(x, 0.0) - jnp.log1p(jnp.exp(-jnp.abs(x)))


def _scan_lanes(x, combine, identity, reverse):
    n = x.shape[-1]
    axis = x.ndim - 1
    lane = lax.broadcasted_iota(jnp.int32, x.shape, axis)
    k = 1
    while k < n:
        if reverse:
            shifted = jnp.where(lane < n - k, pltpu.roll(x, n - k, axis=axis), identity)
        else:
            shifted = jnp.where(lane >= k, pltpu.roll(x, k, axis=axis), identity)
        x = combine(x, shifted)
        k *= 2
    return x


def _mlstm_kernel(mq_ref, mk_ref, mv_ref, mo_ref, cwq_ref, cwk_ref, gr_ref, gb_ref, o_ref,
                  qs, ktf, ktb, vaug, hs, ct, colv):
    S, Dh = mq_ref.shape
    C = MLSTM_CHUNK
    nc = S // C
    halo = CONV_WIDTH // 2

    def conv_silu_rows(src_ref, w_ref):
        x = src_ref[...].astype(F32)
        w = w_ref[...]
        row_s = lax.broadcasted_iota(jnp.int32, (S, Dh), 0)
        acc = x * w[halo:halo + 1, :]
        for j in range(CONV_WIDTH):
            off = j - halo
            if off == 0:
                continue
            shifted = pltpu.roll(x, (-off) % S, axis=0)
            ok = (row_s < S - off) if off > 0 else (row_s >= -off)
            acc = acc + jnp.where(ok, shifted, 0.0) * w[j:j + 1, :]
        return acc * jax.nn.sigmoid(acc)

    qs[...] = conv_silu_rows(mq_ref, cwq_ref).astype(BF16)
    kt = (conv_silu_rows(mk_ref, cwk_ref) * (Dh ** -0.5)).T
    ktf[...] = kt
    ktb[...] = kt.astype(BF16)
    vaug[:, :Dh] = mv_ref[...]
    vaug[:, Dh:] = jnp.ones((S, LANES), BF16)

    gb = gb_ref[0]
    per_dir, col_rows = [], []
    for di in range(2):
        reverse = di == 1
        gi = 2 * di
        log_i = gr_ref[0, 0, gi] + gb[gi:gi + 1, :]
        log_f = _log_sigmoid(gr_ref[0, 0, gi + 1] + gb[gi + 1:gi + 2, :])
        b = _scan_lanes(log_f, jnp.add, 0.0, reverse)
        a = log_i - b
        cm = _scan_lanes(a, jnp.maximum, -jnp.inf, reverse)
        per_dir.append((a, b))
        col_rows += [cm, b]
    pad = jnp.zeros((LANES - 4 * nc, C), F32)
    colv[...] = jnp.concatenate(col_rows + [pad], axis=0).T

    ti = lax.broadcasted_iota(jnp.int32, (C, C), 0)
    si = lax.broadcasted_iota(jnp.int32, (C, C), 1)
    for di in range(2):
        forward = di == 0
        a_all, b_all = per_dir[di]
        feeds = (si <= ti) if forward else (si >= ti)
        ct[...] = jnp.zeros_like(ct)
        m_prev = jnp.zeros((1, 1), F32)
        for step in range(nc):
            c = step if forward else nc - 1 - step
            rows = slice(c * C, (c + 1) * C)
            q = qs[rows, :]
            va = vaug[rows, :]
            a_row = a_all[c:c + 1, :]
            b_last = b_all[c:c + 1, C - 1:C] if forward else b_all[c:c + 1, 0:1]
            col = 2 * nc * di + c
            m_col = jnp.maximum(colv[:, col:col + 1], m_prev)
            b_col = colv[:, col + nc:col + nc + 1]
            e = jnp.exp(jnp.where(feeds, a_row - m_col, -jnp.inf))
            s = jnp.dot(q, ktb[:, rows], preferred_element_type=F32)
            sc = (s * e).astype(BF16)
            w_inter = jnp.exp(m_prev - m_col)
            numa = (jnp.dot(sc, va, preferred_element_type=F32)
                    + w_inter * jnp.dot(q, ct[...].astype(BF16), preferred_element_type=F32))
            den = numa[:, Dh:Dh + 1]
            inv = 1.0 / jnp.maximum(jnp.abs(den), jnp.exp(-(b_col + m_col)))
            h = numa[:, :Dh] * inv
            if forward:
                hs[rows, :] = h
            else:
                hs[rows, :] += h
            if step < nc - 1:
                g_r = b_last + a_row
                m_new = jnp.maximum(b_last + m_prev, jnp.max(g_r, axis=1, keepdims=True))
                decay = jnp.exp(b_last + m_prev - m_new)
                ktw = (ktf[:, rows] * jnp.exp(g_r - m_new)).astype(BF16)
                ct[...] = decay * ct[...] + jnp.dot(ktw, va, preferred_element_type=F32)
                m_prev = m_new

    o_ref[...] = (jax.nn.sigmoid(mo_ref[...].astype(F32)) * hs[...]).astype(o_ref.dtype)


def mlstm(proj, gates, conv_w, gate_b, *, batch, seq, col0):
    H, Dh, C = MLSTM_HEADS, MLSTM_HEAD_DIM, MLSTM_CHUNK
    T = batch * seq
    nc = seq // C
    assert 4 * nc <= LANES
    c0 = col0 // Dh
    g_row = gates[:, :4 * H].reshape(batch, seq, 4, H).transpose(0, 3, 2, 1)
    g_row = g_row.reshape(batch, H, 4, nc, C)
    gb = gate_b.reshape(4, H).T.reshape(H, 4, 1)
    blk = lambda j: pl.BlockSpec((seq, Dh), lambda b, h: (b, c0 + j * H + h))
    return pl.pallas_call(
        _mlstm_kernel,
        grid=(batch, H),
        in_specs=[blk(0), blk(1), blk(2), blk(3),
                  pl.BlockSpec((CONV_WIDTH, Dh), lambda b, h: (0, h)),
                  pl.BlockSpec((CONV_WIDTH, Dh), lambda b, h: (0, H + h)),
                  pl.BlockSpec((1, 1, 4, nc, C), lambda b, h: (b, h, 0, 0, 0)),
                  pl.BlockSpec((1, 4, 1), lambda b, h: (h, 0, 0))],
        out_specs=pl.BlockSpec((seq, Dh), lambda b, h: (b, h)),
        out_shape=jax.ShapeDtypeStruct((T, H * Dh), BF16),
        scratch_shapes=[pltpu.VMEM((seq, Dh), BF16),
                        pltpu.VMEM((Dh, seq), F32),
                        pltpu.VMEM((Dh, seq), BF16),
                        pltpu.VMEM((seq, Dh + LANES), BF16),
                        pltpu.VMEM((seq, Dh), F32),
                        pltpu.VMEM((Dh, Dh + LANES), F32),
                        pltpu.VMEM((C, LANES), F32)],
        compiler_params=_params("parallel", "parallel"),
        name="bidir_mlstm",
    )(proj, proj, proj, proj, conv_w, conv_w, g_row, gb)


def _router_kernel(x_ref, wr_ref, br_ref, g_ref, rank_ref, cnt_ref):
    tm = x_ref.shape[0]

    @pl.when(pl.program_id(0) == 0)
    def _():
        cnt_ref[...] = jnp.zeros_like(cnt_ref)

    x = x_ref[...]
    x_hi = x.astype(BF16)
    x_lo = (x - x_hi.astype(F32)).astype(BF16)
    w_hi, w_lo = wr_ref[0], wr_ref[1]
    logits = (jnp.dot(x_hi, w_hi, preferred_element_type=F32)
              + (jnp.dot(x_hi, w_lo, preferred_element_type=F32)
                 + jnp.dot(x_lo, w_hi, preferred_element_type=F32))) + br_ref[...]
    lane = lax.broadcasted_iota(jnp.int32, logits.shape, 1)
    lg = jnp.where(lane < N_EXPERTS, logits, -jnp.inf)
    v1 = jnp.max(lg, axis=1, keepdims=True)
    i1 = jnp.min(jnp.where(lg == v1, lane, LANES), axis=1, keepdims=True)
    lg2 = jnp.where(lane == i1, -jnp.inf, lg)
    v2 = jnp.max(lg2, axis=1, keepdims=True)
    i2 = jnp.min(jnp.where(lg2 == v2, lane, LANES), axis=1, keepdims=True)
    e = jnp.exp(v2 - v1)
    p1 = 1.0 / (1.0 + e)
    p2 = e / (1.0 + e)
    g_ref[...] = jnp.where(lane == i1, p1, 0.0) + jnp.where(lane == i2, p2, 0.0)
    sel = (lane == i1) | (lane == i2)
    self_ = jnp.where(sel, 1.0, 0.0)
    row = lax.broadcasted_iota(jnp.int32, (tm, tm), 0)
    col = lax.broadcasted_iota(jnp.int32, (tm, tm), 1)
    earlier = jnp.where(col < row, 1.0, 0.0).astype(BF16)
    raw = (jnp.dot(earlier, self_.astype(BF16), preferred_element_type=F32)
           + cnt_ref[...]).astype(jnp.int32)
    rank_ref[...] = jnp.where(sel, raw, -1 - raw)
    cnt_ref[...] += jnp.sum(self_, axis=0, keepdims=True)


def router(x, wr, br, *, tm):
    T, D = x.shape
    wr_p = jnp.zeros((D, LANES), F32).at[:, :N_EXPERTS].set(wr)
    wr_hi = wr_p.astype(BF16)
    wr_p = jnp.stack([wr_hi, (wr_p - wr_hi.astype(F32)).astype(BF16)])
    br_p = jnp.zeros((1, LANES), F32).at[0, :N_EXPERTS].set(br)
    return pl.pallas_call(
        _router_kernel,
        grid=(T // tm,),
        in_specs=[pl.BlockSpec((tm, D), lambda i: (i, 0)),
                  pl.BlockSpec((2, D, LANES), lambda i: (0, 0, 0)),
                  pl.BlockSpec((1, LANES), lambda i: (0, 0))],
        out_specs=[pl.BlockSpec((tm, LANES), lambda i: (i, 0)),
                   pl.BlockSpec((tm, LANES), lambda i: (i, 0))],
        out_shape=[jax.ShapeDtypeStruct((T, LANES), F32),
                   jax.ShapeDtypeStruct((T, LANES), jnp.int32)],
        scratch_shapes=[pltpu.VMEM((1, LANES), F32)],
        compiler_params=_params("arbitrary"),
        name="moe_router",
    )(x, wr_p, br_p)


MOE_TILE = 512
MOE_TOKEN_BLOCK = 512
MOE_ROW_ALIGN = 16
MOE_STAGE_ROWS = 2 * MOE_TOKEN_BLOCK + N_EXPERTS * MOE_ROW_ALIGN


def _round_up(x, m):
    return (x + m - 1) // m * m


def _moe_plan(rank, gates, *, tile, tb):
    T = rank.shape[0]
    E, A = N_EXPERTS, MOE_ROW_ALIGN
    nb = T // tb
    r = rank[:, :E]
    sel = r >= 0
    raw = jnp.where(sel, r, -1 - r)
    cb = raw[::tb]
    counts = raw[-1] + sel[-1].astype(jnp.int32)
    cnt = jnp.concatenate([cb[1:], counts[None, :]], axis=0) - cb
    padn = _round_up(cnt, A)
    size = jnp.sum(padn, axis=0)
    gsize = _round_up(size, tile)
    ends = jnp.cumsum(gsize)
    start = ends - gsize
    lo = start[None, :] + jnp.cumsum(padn, axis=0) - padn
    off = jnp.cumsum(padn, axis=1) - padn
    within = raw - jnp.repeat(cb, tb, axis=0)
    local = jnp.repeat(off, tb, axis=0) + within
    eid = jnp.arange(E, dtype=jnp.int32)[None, :]
    slot_e = jnp.stack([jnp.min(jnp.where(sel, eid, E - 1), axis=1),
                        jnp.max(jnp.where(sel, eid, 0), axis=1)], axis=1)
    slot_row = jnp.take_along_axis(local, slot_e, axis=1).astype(jnp.int32)
    slot_gate = jnp.take_along_axis(gates[:, :E], slot_e, axis=1)
    n_tiles = -(-(2 * T + nb * E * (A - 1) + E * (tile - 1)) // tile)
    tile_starts = jnp.arange(n_tiles, dtype=jnp.int32) * tile
    count_le = lambda keys, x: jnp.sum((keys[None, :] <= x[:, None]).astype(jnp.int32), axis=1)
    tile_expert = jnp.clip(count_le(ends, tile_starts), 0, E - 1).astype(jnp.int32)
    tile_valid = (tile_starts < ends[-1]).astype(jnp.int32)
    tile_new = jnp.concatenate([jnp.ones((1,), jnp.int32),
                                (tile_expert[1:] != tile_expert[:-1]).astype(jnp.int32)])
    flat = lambda a: a.reshape(-1).astype(jnp.int32)
    ranges = (flat(lo), flat(padn), flat(off))
    tails = ((start + size).astype(jnp.int32), (gsize - size).astype(jnp.int32),
             (ends[-1:] // tile).astype(jnp.int32))
    return slot_row, slot_gate, ranges, tails, (tile_expert, tile_valid, tile_new), n_tiles


def _for_each_piece(n, max_piece, fn):
    o = jnp.int32(0)
    size = max_piece
    while size >= MOE_ROW_ALIGN:
        take = (n & size) != 0

        @pl.when(take)
        def _(o=o, size=size):
            fn(pl.multiple_of(o, MOE_ROW_ALIGN), size)

        o = o + jnp.where(take, size, 0)
        size //= 2


def _gather_kernel(lo_ref, len_ref, off_ref, tail_lo_ref, tail_len_ref, used_tiles_ref,
                   x_ref, rows_ref, xs_hbm, stage, sems):
    b = pl.program_id(0)
    last = pl.num_programs(0) - 1
    slot = b % 2
    tb = x_ref.shape[0]

    def range_copies(blk, slot, act):
        for e in range(N_EXPERTS):
            k = blk * N_EXPERTS + e
            lo = pl.multiple_of(lo_ref[k], MOE_ROW_ALIGN)
            off = pl.multiple_of(off_ref[k], MOE_ROW_ALIGN)
            _for_each_piece(len_ref[k], tb, lambda o, size: act(pltpu.make_async_copy(
                stage.at[slot, pl.ds(off + o, size)], xs_hbm.at[pl.ds(lo + o, size)],
                sems.at[slot])))

    start = lambda c: c.start()
    wait = lambda c: c.wait()

    @pl.when(b >= 2)
    def _():
        range_copies(b - 2, slot, wait)

    rid = lax.broadcasted_iota(jnp.int32, (stage.shape[1], tb), 0)
    rows = rows_ref[...]
    onehot = jnp.where((rows[0:1, :] == rid) | (rows[1:2, :] == rid), 1.0, 0.0).astype(BF16)
    stage[slot] = jnp.dot(onehot, x_ref[...], preferred_element_type=F32).astype(BF16)
    range_copies(b, slot, start)

    @pl.when(b == last)
    def _():
        @pl.when(b >= 1)
        def _():
            range_copies(b - 1, 1 - slot, wait)

        range_copies(b, slot, wait)

        stage[slot, 0:MOE_TILE, :] = jnp.zeros((MOE_TILE, stage.shape[2]), BF16)

        def tail_copies(act):
            for e in range(N_EXPERTS):
                lo = pl.multiple_of(tail_lo_ref[e], MOE_ROW_ALIGN)
                _for_each_piece(tail_len_ref[e], MOE_TILE // 2, lambda o, size: act(
                    pltpu.make_async_copy(stage.at[slot, pl.ds(0, size)],
                                          xs_hbm.at[pl.ds(lo + o, size)], sems.at[slot])))

        tail_copies(start)
        tail_copies(wait)

        def zero_tile(t, carry):
            cp = pltpu.make_async_copy(
                stage.at[slot, pl.ds(0, MOE_TILE)],
                xs_hbm.at[pl.ds(pl.multiple_of(t * MOE_TILE, MOE_TILE), MOE_TILE)], sems.at[slot])
            cp.start()
            cp.wait()
            return carry

        lax.fori_loop(used_tiles_ref[0], xs_hbm.shape[0] // MOE_TILE, zero_tile, 0)


def moe_gather(xb, slot_row_t, ranges, tails, *, n_tiles, tile, tb):
    T, D = xb.shape
    grid_spec = pltpu.PrefetchScalarGridSpec(
        num_scalar_prefetch=6, grid=(T // tb,),
        in_specs=[pl.BlockSpec((tb, D), lambda b, *_: (b, 0)),
                  pl.BlockSpec((2, tb), lambda b, *_: (0, b))],
        out_specs=pl.BlockSpec(memory_space=pl.ANY),
        scratch_shapes=[pltpu.VMEM((2, MOE_STAGE_ROWS, D), BF16), pltpu.SemaphoreType.DMA((2,))])
    return pl.pallas_call(
        _gather_kernel, grid_spec=grid_spec,
        out_shape=jax.ShapeDtypeStruct((n_tiles * tile, D), BF16),
        compiler_params=_params("arbitrary"), name="moe_gather",
    )(*ranges, *tails, xb, slot_row_t)


def _expert_swiglu_kernel(te, tv, tnew, x_ref, w1_ref, w3_ref, o_ref, w1b, w3b):
    i = pl.program_id(1)

    @pl.when(tnew[i] == 1)
    def _():
        w1b[...] = w1_ref[0, 0].astype(BF16)
        w3b[...] = w3_ref[0, 0].astype(BF16)

    @pl.when(tv[i] == 1)
    def _():
        a = x_ref[...]
        h1 = jnp.dot(a, w1b[...], preferred_element_type=F32)
        h3 = jnp.dot(a, w3b[...], preferred_element_type=F32)
        o_ref[...] = (h1 * jax.nn.sigmoid(h1) * h3).astype(o_ref.dtype)

    @pl.when(tv[i] == 0)
    def _():
        o_ref[...] = jnp.zeros_like(o_ref)


def expert_swiglu(xs, w1, w3, layer, tile_expert, tile_valid, tile_new, *, tile, tf):
    R, D = xs.shape
    F = w1.shape[3]
    n_tiles = R // tile
    wspec = pl.BlockSpec((1, 1, D, tf), lambda f, i, te, tv, tn: (layer, te[i], 0, f),
                         pipeline_mode=pl.Buffered(1))
    grid_spec = pltpu.PrefetchScalarGridSpec(
        num_scalar_prefetch=3, grid=(F // tf, n_tiles),
        in_specs=[pl.BlockSpec((tile, D), lambda f, i, te, tv, tn: (i, 0)), wspec, wspec],
        out_specs=pl.BlockSpec((tile, tf), lambda f, i, te, tv, tn: (i, f)),
        scratch_shapes=[pltpu.VMEM((D, tf), BF16), pltpu.VMEM((D, tf), BF16)])
    return pl.pallas_call(
        _expert_swiglu_kernel, grid_spec=grid_spec,
        out_shape=jax.ShapeDtypeStruct((R, F), BF16),
        compiler_params=_params("parallel", "arbitrary"), name="expert_swiglu",
    )(tile_expert, tile_valid, tile_new, xs, w1, w3)


def _expert_w2_kernel(te, tv, tnew, h_ref, w2_ref, o_ref, w2b):
    i = pl.program_id(0)

    @pl.when(tnew[i] == 1)
    def _():
        w2b[...] = w2_ref[0, 0].astype(BF16)

    @pl.when(tv[i] == 1)
    def _():
        o_ref[...] = jnp.dot(h_ref[...], w2b[...], preferred_element_type=F32).astype(o_ref.dtype)

    @pl.when(tv[i] == 0)
    def _():
        o_ref[...] = jnp.zeros_like(o_ref)


def expert_w2(h, w2, layer, tile_expert, tile_valid, tile_new, *, tile):
    R, F = h.shape
    D = w2.shape[3]
    grid_spec = pltpu.PrefetchScalarGridSpec(
        num_scalar_prefetch=3, grid=(R // tile,),
        in_specs=[pl.BlockSpec((tile, F), lambda i, te, tv, tn: (i, 0)),
                  pl.BlockSpec((1, 1, F, D), lambda i, te, tv, tn: (layer, te[i], 0, 0),
                               pipeline_mode=pl.Buffered(1))],
        out_specs=pl.BlockSpec((tile, D), lambda i, te, tv, tn: (i, 0)),
        scratch_shapes=[pltpu.VMEM((F, D), BF16)])
    return pl.pallas_call(
        _expert_w2_kernel, grid_spec=grid_spec,
        out_shape=jax.ShapeDtypeStruct((R, D), BF16),
        compiler_params=_params("arbitrary"), name="expert_w2",
    )(tile_expert, tile_valid, tile_new, h, w2)


def _combine_kernel(lo_ref, len_ref, off_ref, y_hbm, rows_ref, gate_ref, res_ref, lng_ref, lnb_ref,
                    o_ref, ob_ref, stage, sems, *, alpha):
    b = pl.program_id(0)
    slot = b % 2
    tb = rows_ref.shape[0]

    def range_copies(blk, slot, act):
        for e in range(N_EXPERTS):
            k = blk * N_EXPERTS + e
            lo = pl.multiple_of(lo_ref[k], MOE_ROW_ALIGN)
            off = pl.multiple_of(off_ref[k], MOE_ROW_ALIGN)
            _for_each_piece(len_ref[k], tb, lambda o, size: act(pltpu.make_async_copy(
                y_hbm.at[pl.ds(lo + o, size)], stage.at[slot, pl.ds(off + o, size)],
                sems.at[slot])))

    @pl.when(b == 0)
    def _():
        stage[...] = jnp.zeros_like(stage)
        range_copies(0, 0, lambda c: c.start())

    @pl.when(b + 1 < pl.num_programs(0))
    def _():
        range_copies(b + 1, 1 - slot, lambda c: c.start())

    range_copies(b, slot, lambda c: c.wait())

    cid = lax.broadcasted_iota(jnp.int32, (tb, stage.shape[1]), 1)
    rows, gate = rows_ref[...], gate_ref[...]
    weights = (jnp.where(rows[:, 0:1] == cid, gate[:, 0:1], 0.0)
               + jnp.where(rows[:, 1:2] == cid, gate[:, 1:2], 0.0)).astype(BF16)
    f = jnp.dot(weights, stage[slot], preferred_element_type=F32)
    out = _layer_norm_rows(alpha * res_ref[...] + f, lng_ref[...], lnb_ref[...])
    o_ref[...] = out
    ob_ref[...] = out.astype(BF16)


def moe_combine_ln(y, slot_row, slot_gate, res, g, beta, ranges, *, tb, alpha):
    T, D = res.shape
    tok = lambda b, *_: (b, 0)
    const = lambda b, *_: (0, 0)
    grid_spec = pltpu.PrefetchScalarGridSpec(
        num_scalar_prefetch=3, grid=(T // tb,),
        in_specs=[pl.BlockSpec(memory_space=pl.ANY),
                  pl.BlockSpec((tb, 2), tok), pl.BlockSpec((tb, 2), tok),
                  pl.BlockSpec((tb, D), tok),
                  pl.BlockSpec((1, D), const), pl.BlockSpec((1, D), const)],
        out_specs=[pl.BlockSpec((tb, D), tok), pl.BlockSpec((tb, D), tok)],
        scratch_shapes=[pltpu.VMEM((2, MOE_STAGE_ROWS, D), BF16), pltpu.SemaphoreType.DMA((2,))])
    return pl.pallas_call(
        functools.partial(_combine_kernel, alpha=alpha), grid_spec=grid_spec,
        out_shape=[jax.ShapeDtypeStruct((T, D), F32), jax.ShapeDtypeStruct((T, D), BF16)],
        compiler_params=_params("arbitrary"), name="moe_combine_ln",
    )(*ranges, y, slot_row, slot_gate, res, g.reshape(1, D), beta.reshape(1, D))


def moe_layer(xf, xb, wr, br, w1, w3, w2, layer, ln_g, ln_b, *, alpha):
    tile, tb = MOE_TILE, MOE_TOKEN_BLOCK
    gates, rank = router(xf, wr, br, tm=1024)
    slot_row, slot_gate, ranges, tails, tiles, n_tiles = _moe_plan(rank, gates, tile=tile, tb=tb)
    xs = moe_gather(xb, slot_row.T, ranges, tails, n_tiles=n_tiles, tile=tile, tb=tb)
    h = expert_swiglu(xs, w1, w3, layer, *tiles, tile=tile, tf=1408)
    y = expert_w2(h, w2, layer, *tiles, tile=tile)
    return moe_combine_ln(y, slot_row, slot_gate, xf, ln_g, ln_b, ranges, tb=tb, alpha=alpha)


def kernel(x, w_in, w_out, conv_w, gate_b, rpb_table, ln_g, ln_b, dense_w1, dense_w3,
           dense_w2, router_w, router_b, moe_w1, moe_w3, moe_w2):
    B, S, D = x.shape
    T = B * S
    depth = w_in.shape[0]
    alpha = (2 * depth) ** 0.25
    attn_w = ATTN_HEADS * ATTN_HEAD_DIM
    mlstm_w = MLSTM_HEADS * MLSTM_HEAD_DIM
    main_cols = 3 * attn_w + 4 * mlstm_w

    bias_tiles = _attn_bias_tiles(rpb_table)
    xf = x.reshape(T, D)
    xb = xf.astype(BF16)
    w_gate = jnp.pad(lax.slice_in_dim(w_in, main_cols, w_in.shape[2], axis=2),
                     ((0, 0), (0, 0), (0, LANES - 4 * MLSTM_HEADS))).astype(BF16)
    w_in_t = jnp.swapaxes(w_in, 1, 2)
    for l in range(depth):
        proj = in_projection(xb, w_in_t, l, n_cols=main_cols, tm=1024, tn=1024)
        gates = matmul(xb, w_gate[l], tm=2048, tn=LANES, out_dtype=F32, name="gate_proj")
        attn = attention(proj, bias_tiles, batch=B, seq=S)
        rec = mlstm(proj, gates, conv_w[l], gate_b[l], batch=B, seq=S, col0=3 * attn_w)
        wo = w_out[l].astype(BF16)
        xf, xb = matmul_residual_ln([attn, rec], [wo[:attn_w], wo[attn_w:]], xf,
                                    ln_g[l, 0], ln_b[l, 0], tm=512, sub=256, alpha=alpha,
                                    name="out_proj_ln")
        j = l // 2
        if l % 2 == 0:
            hmid = swiglu_matmul(xb, dense_w1[j].astype(BF16), dense_w3[j].astype(BF16),
                                 tm=1024, tf=1408, name="dense_swiglu")
            xf, xb = matmul_residual_ln([hmid], [dense_w2[j].astype(BF16)], xf,
                                        ln_g[l, 1], ln_b[l, 1], tm=256, sub=256, alpha=alpha,
                                        name="dense_w2_ln")
        else:
            xf, xb = moe_layer(xf, xb, router_w[j], router_b[j], moe_w1, moe_w3, moe_w2, j,
                               ln_g[l, 1], ln_b[l, 1], alpha=alpha)
    return xf.reshape(B, S, D)
```

```python
import functools
import math

import jax
import jax.numpy as jnp
from jax import lax
from jax.experimental import pallas as pl
from jax.experimental.pallas import tpu as pltpu

F32 = jnp.float32
BF16 = jnp.bfloat16

ATTN_HEADS = 8
ATTN_HEAD_DIM = 128
MLSTM_HEADS = 4
MLSTM_HEAD_DIM = 256
DILATED_BRANCHES = ((128, 1), (512, 4), (2048, 16))
BAND_HALF = 64
REL_BUCKETS = 32
REL_MAX_DIST = 1024
NEG_INF = -1e30
CONV_WIDTH = 5
N_EXPERTS = 8
LN_EPS = 1e-5

V7X_VMEM_BYTES = 64 * 1024 * 1024
VMEM_LIMIT_BYTES = V7X_VMEM_BYTES - 8 * 1024 * 1024
LANES = 128

ATTN_Q_BLOCK = 128
ATTN_BATCH = 8
ATTN_DEINTERLEAVE = 4
MLSTM_CHUNK = 256


def _params(*semantics):
    return pltpu.CompilerParams(dimension_semantics=semantics,
                                vmem_limit_bytes=VMEM_LIMIT_BYTES)


def _mm_kernel(a_ref, b_ref, o_ref):
    o_ref[...] = jnp.dot(a_ref[...], b_ref[...],
                         preferred_element_type=F32).astype(o_ref.dtype)


def matmul(a, b, *, tm, tn, out_dtype, name):
    M, K = a.shape
    N = b.shape[1]
    assert M % tm == 0 and N % tn == 0
    return pl.pallas_call(
        _mm_kernel,
        grid=(M // tm, N // tn),
        in_specs=[pl.BlockSpec((tm, K), lambda i, j: (i, 0)),
                  pl.BlockSpec((K, tn), lambda i, j: (0, j))],
        out_specs=pl.BlockSpec((tm, tn), lambda i, j: (i, j)),
        out_shape=jax.ShapeDtypeStruct((M, N), out_dtype),
        compiler_params=_params("parallel", "parallel"),
        name=name,
    )(a, b)


def _in_proj_kernel(a_ref, w_ref, o_ref, wb_ref):
    @pl.when(pl.program_id(1) == 0)
    def _():
        wb_ref[...] = w_ref[0].T.astype(BF16)

    o_ref[...] = jnp.dot(a_ref[...], wb_ref[...],
                         preferred_element_type=F32).astype(o_ref.dtype)


def in_projection(a, w_stack_t, layer, *, n_cols, tm, tn):
    M, K = a.shape
    assert M % tm == 0 and n_cols % tn == 0
    return pl.pallas_call(
        _in_proj_kernel,
        grid=(n_cols // tn, M // tm),
        in_specs=[pl.BlockSpec((tm, K), lambda j, i: (i, 0)),
                  pl.BlockSpec((1, tn, K), lambda j, i: (layer, j, 0))],
        out_specs=pl.BlockSpec((tm, tn), lambda j, i: (i, j)),
        out_shape=jax.ShapeDtypeStruct((M, n_cols), BF16),
        scratch_shapes=[pltpu.VMEM((K, tn), BF16)],
        compiler_params=_params("parallel", "arbitrary"),
        name="in_proj",
    )(a, w_stack_t)


def _swiglu_kernel(a_ref, w1_ref, w3_ref, o_ref):
    a = a_ref[...]
    h1 = jnp.dot(a, w1_ref[0], preferred_element_type=F32)
    h3 = jnp.dot(a, w3_ref[0], preferred_element_type=F32)
    o_ref[...] = (h1 * jax.nn.sigmoid(h1) * h3).astype(o_ref.dtype)


def swiglu_matmul(a, w1, w3, layer, *, tm, tf, name):
    M, K = a.shape
    F = w1.shape[2]
    assert M % tm == 0 and F % tf == 0
    return pl.pallas_call(
        _swiglu_kernel,
        grid=(M // tm, F // tf),
        in_specs=[pl.BlockSpec((tm, K), lambda i, j: (i, 0)),
                  pl.BlockSpec((1, K, tf), lambda i, j: (layer, 0, j)),
                  pl.BlockSpec((1, K, tf), lambda i, j: (layer, 0, j))],
        out_specs=pl.BlockSpec((tm, tf), lambda i, j: (i, j)),
        out_shape=jax.ShapeDtypeStruct((M, F), BF16),
        compiler_params=_params("parallel", "parallel"),
        name=name,
    )(a, w1, w3)


def _layer_norm_rows(z, g, b):
    mu = jnp.mean(z, axis=-1, keepdims=True)
    zc = z - mu
    var = jnp.mean(zc * zc, axis=-1, keepdims=True)
    return zc * lax.rsqrt(var + LN_EPS) * g + b


def _mm_res_ln_kernel(*refs, n_in, sub, alpha):
    a_refs, b_refs = refs[:n_in], refs[n_in:2 * n_in]
    res_ref, lng_ref, lnb_ref, o_ref, ob_ref = refs[2 * n_in:]
    for s0 in range(0, o_ref.shape[0], sub):
        rows = slice(s0, s0 + sub)
        y = None
        for a_ref, b_ref in zip(a_refs, b_refs):
            part = jnp.dot(a_ref[rows, :], b_ref[0], preferred_element_type=F32)
            y = part if y is None else y + part
        out = _layer_norm_rows(alpha * res_ref[rows, :] + y, lng_ref[...], lnb_ref[...])
        o_ref[rows, :] = out
        ob_ref[rows, :] = out.astype(BF16)


def matmul_residual_ln(a_list, b_stack, layer, res, g, beta, *, tm, sub, alpha, name):
    M, N = res.shape
    assert M % tm == 0 and tm % sub == 0
    widths = [a.shape[1] for a in a_list]
    assert all(w == widths[0] for w in widths) and sum(widths) == b_stack.shape[1]
    row = lambda i: (i, 0)
    const = lambda i: (0, 0)
    in_specs = ([pl.BlockSpec((tm, w), row) for w in widths]
                + [pl.BlockSpec((1, w, N), lambda i, kb=kb: (layer, kb, 0),
                                pipeline_mode=pl.Buffered(1)) for kb, w in enumerate(widths)]
                + [pl.BlockSpec((tm, N), row), pl.BlockSpec((1, N), const),
                   pl.BlockSpec((1, N), const)])
    return pl.pallas_call(
        functools.partial(_mm_res_ln_kernel, n_in=len(a_list), sub=sub, alpha=alpha),
        grid=(M // tm,),
        in_specs=in_specs,
        out_specs=[pl.BlockSpec((tm, N), row), pl.BlockSpec((tm, N), row)],
        out_shape=[jax.ShapeDtypeStruct((M, N), F32), jax.ShapeDtypeStruct((M, N), BF16)],
        compiler_params=_params("parallel"),
        name=name,
    )(*a_list, *([b_stack] * len(a_list)), res, g.reshape(1, N), beta.reshape(1, N))


def _t5_bucket(rel):
    half = REL_BUCKETS // 2
    max_exact = half // 2
    n = jnp.abs(rel)
    large = max_exact + (jnp.log(jnp.maximum(n, 1).astype(F32) / max_exact)
                         / math.log(REL_MAX_DIST / max_exact)
                         * (half - max_exact)).astype(jnp.int32)
    large = jnp.minimum(large, half - 1)
    return jnp.where(rel > 0, half, 0) + jnp.where(n < max_exact, n, large)


_ATTN_WINDOW_DELTAS = (0, -BAND_HALF, -2 * BAND_HALF)


def _attn_bias_tiles(table):
    QB, W, N = ATTN_Q_BLOCK, 2 * ATTN_Q_BLOCK, 8 * ATTN_Q_BLOCK
    base = -min(_ATTN_WINDOW_DELTAS)
    j = jnp.arange(N)
    m = jnp.where(j < N // 2, j, j - N) - base
    tiles = []
    for window, dilation in DILATED_BRANCHES:
        assert window // (2 * dilation) == BAND_HALF
        vals = jnp.where((jnp.abs(m) <= BAND_HALF)[:, None],
                         table[_t5_bucket(m * dilation)].astype(F32), NEG_INF)
        flat = jnp.tile(vals.T, (1, QB))[:, :QB * (N - 1)]
        circ = flat.reshape(-1, QB, N - 1)
        tiles.append(jnp.stack([circ[:, :, base + d:base + d + W] for d in _ATTN_WINDOW_DELTAS]))
    return jnp.stack(tiles)


def _band_blocks(q, k, v, bias):
    s = jnp.einsum("nqd,nkd->nqk", q, k, preferred_element_type=F32)
    s = s * (ATTN_HEAD_DIM ** -0.5) + bias
    m = jnp.max(s, axis=-1, keepdims=True)
    p = jnp.exp(s - m)
    l = jnp.sum(p, axis=-1, keepdims=True)
    o = jnp.einsum("nqk,nkd->nqd", p.astype(BF16), v, preferred_element_type=F32) / l
    return o, m + jnp.log(l)


def _attn_kernel(q_ref, k_ref, v_ref, bias_ref, o_ref, qf, kf, vf, og, lg):
    S = q_ref.shape[0]
    QB = ATTN_Q_BLOCK
    DS = ATTN_DEINTERLEAVE
    for src, dst in ((q_ref, qf), (k_ref, kf), (v_ref, vf)):
        og[0] = src[...].astype(F32)
        for r in range(DS):
            dst[r] = og[0, pl.ds(r, S // DS, stride=DS), :]

    for g, (_, d) in enumerate(DILATED_BRANCHES):
        assert d in (1, DS, DS * DS)
        L = S // d
        assert L % QB == 0 and (L == QB or L >= 3 * QB)
        W = QB if L == QB else 2 * QB
        blocks = []
        for r in range(d):
            for n in range(L // QB):
                q0 = n * QB
                ks = 0 if L == QB else min(max(q0 - BAND_HALF, 0), L - 2 * QB)
                blocks.append((r + d * q0, r + d * ks, _ATTN_WINDOW_DELTAS.index(ks - q0)))

        def rows(ref32, ref16, start, n):
            if d == 1:
                return ref16[start:start + n, :]
            r, t = start % DS, start // DS
            if d == DS:
                return ref32[r, t:t + n, :].astype(BF16)
            return ref32[r, pl.ds(t, n, stride=DS), :].astype(BF16)

        for i0 in range(0, len(blocks), ATTN_BATCH):
            batch = blocks[i0:i0 + ATTN_BATCH]
            qb = jnp.stack([rows(qf, q_ref, qs, QB) for qs, _, _ in batch])
            kb = jnp.stack([rows(kf, k_ref, ks, W) for _, ks, _ in batch])
            vb = jnp.stack([rows(vf, v_ref, ks, W) for _, ks, _ in batch])
            bias = jnp.stack([bias_ref[g, case, 0][:, :W] for _, _, case in batch])
            o, lse = _band_blocks(qb, kb, vb, bias)
            for j, (qs, _, _) in enumerate(batch):
                idx = slice(qs, qs + QB) if d == 1 else pl.ds(qs, QB, stride=d)
                og[g, idx, :] = o[j]
                lg[g, idx, :] = jnp.broadcast_to(lse[j], (QB, LANES))

    l0, l1, l2 = lg[0], lg[1], lg[2]
    mx = jnp.maximum(jnp.maximum(l0, l1), l2)
    e0, e1, e2 = jnp.exp(l0 - mx), jnp.exp(l1 - mx), jnp.exp(l2 - mx)
    out = (e0 * og[0] + e1 * og[1] + e2 * og[2]) / (e0 + e1 + e2)
    o_ref[...] = out.astype(o_ref.dtype)


def attention(proj, bias_tiles, *, batch, seq):
    H, Dh = ATTN_HEADS, ATTN_HEAD_DIM
    assert Dh == LANES
    T = batch * seq
    blk = lambda off: pl.BlockSpec((seq, Dh), lambda h, b: (b, off + h))
    return pl.pallas_call(
        _attn_kernel,
        grid=(H, batch),
        in_specs=[blk(0), blk(H), blk(2 * H),
                  pl.BlockSpec((3, 3, 1, ATTN_Q_BLOCK, 2 * ATTN_Q_BLOCK),
                               lambda h, b: (0, 0, h, 0, 0))],
        out_specs=pl.BlockSpec((seq, Dh), lambda h, b: (b, h)),
        out_shape=jax.ShapeDtypeStruct((T, H * Dh), BF16),
        scratch_shapes=[pltpu.VMEM((ATTN_DEINTERLEAVE, seq // ATTN_DEINTERLEAVE, Dh), F32)] * 3
                       + [pltpu.VMEM((3, seq, Dh), F32)] * 2,
        compiler_params=_params("parallel", "parallel"),
        name="dilated_attention",
    )(proj, proj, proj, bias_tiles)


def _log_sigmoid(x):
    return jnp.minimum(x, 0.0) - jnp.log1p(jnp.exp(-jnp.abs(x)))


def _scan_lanes(x, combine, identity, reverse):
    n = x.shape[-1]
    axis = x.ndim - 1
    lane = lax.broadcasted_iota(jnp.int32, x.shape, axis)
    k = 1
    while k < n:
        if reverse:
            shifted = jnp.where(lane < n - k, pltpu.roll(x, n - k, axis=axis), identity)
        else:
            shifted = jnp.where(lane >= k, pltpu.roll(x, k, axis=axis), identity)
        x = combine(x, shifted)
        k *= 2
    return x


def _mlstm_kernel(mq_ref, mk_ref, mv_ref, mo_ref, cwq_ref, cwk_ref, gr_ref, gb_ref, o_ref,
                  qs, ktf, ktb, vaug, hs, ct, colv):
    S, Dh = mq_ref.shape
    C = MLSTM_CHUNK
    nc = S // C
    halo = CONV_WIDTH // 2

    def conv_silu_rows(src_ref, w_ref):
        x = src_ref[...].astype(F32)
        w = w_ref[...]
        row_s = lax.broadcasted_iota(jnp.int32, (S, Dh), 0)
        acc = x * w[halo:halo + 1, :]
        for j in range(CONV_WIDTH):
            off = j - halo
            if off == 0:
                continue
            shifted = pltpu.roll(x, (-off) % S, axis=0)
            ok = (row_s < S - off) if off > 0 else (row_s >= -off)
            acc = acc + jnp.where(ok, shifted, 0.0) * w[j:j + 1, :]
        return acc * jax.nn.sigmoid(acc)

    qs[...] = conv_silu_rows(mq_ref, cwq_ref).astype(BF16)
    kt = (conv_silu_rows(mk_ref, cwk_ref) * (Dh ** -0.5)).T
    ktf[...] = kt
    ktb[...] = kt.astype(BF16)
    vaug[:, :Dh] = mv_ref[...]
    vaug[:, Dh:] = jnp.ones((S, LANES), BF16)

    gb = gb_ref[0]
    per_dir, col_rows = [], []
    for di in range(2):
        reverse = di == 1
        gi = 2 * di
        log_i = gr_ref[0, 0, gi] + gb[gi:gi + 1, :]
        log_f = _log_sigmoid(gr_ref[0, 0, gi + 1] + gb[gi + 1:gi + 2, :])
        b = _scan_lanes(log_f, jnp.add, 0.0, reverse)
        a = log_i - b
        cm = _scan_lanes(a, jnp.maximum, -jnp.inf, reverse)
        per_dir.append((a, b))
        col_rows += [cm, b]
    pad = jnp.zeros((LANES - 4 * nc, C), F32)
    colv[...] = jnp.concatenate(col_rows + [pad], axis=0).T

    ti = lax.broadcasted_iota(jnp.int32, (C, C), 0)
    si = lax.broadcasted_iota(jnp.int32, (C, C), 1)
    for di in range(2):
        forward = di == 0
        a_all, b_all = per_dir[di]
        feeds = (si <= ti) if forward else (si >= ti)
        ct[...] = jnp.zeros_like(ct)
        m_prev = jnp.zeros((1, 1), F32)
        for step in range(nc):
            c = step if forward else nc - 1 - step
            rows = slice(c * C, (c + 1) * C)
            q = qs[rows, :]
            va = vaug[rows, :]
            a_row = a_all[c:c + 1, :]
            b_last = b_all[c:c + 1, C - 1:C] if forward else b_all[c:c + 1, 0:1]
            col = 2 * nc * di + c
            m_col = jnp.maximum(colv[:, col:col + 1], m_prev)
            b_col = colv[:, col + nc:col + nc + 1]
            e = jnp.exp(jnp.where(feeds, a_row - m_col, -jnp.inf))
            s = jnp.dot(q, ktb[:, rows], preferred_element_type=F32)
            sc = (s * e).astype(BF16)
            w_inter = jnp.exp(m_prev - m_col)
            numa = (jnp.dot(sc, va, preferred_element_type=F32)
                    + w_inter * jnp.dot(q, ct[...].astype(BF16), preferred_element_type=F32))
            den = numa[:, Dh:Dh + 1]
            inv = 1.0 / jnp.maximum(jnp.abs(den), jnp.exp(-(b_col + m_col)))
            h = numa[:, :Dh] * inv
            if forward:
                hs[rows, :] = h
            else:
                hs[rows, :] += h
            if step < nc - 1:
                g_r = b_last + a_row
                m_new = jnp.maximum(b_last + m_prev, jnp.max(g_r, axis=1, keepdims=True))
                decay = jnp.exp(b_last + m_prev - m_new)
                ktw = (ktf[:, rows] * jnp.exp(g_r - m_new)).astype(BF16)
                ct[...] = decay * ct[...] + jnp.dot(ktw, va, preferred_element_type=F32)
                m_prev = m_new

    o_ref[...] = (jax.nn.sigmoid(mo_ref[...].astype(F32)) * hs[...]).astype(o_ref.dtype)


def mlstm(proj, gates, conv_w, gate_b, *, batch, seq, col0):
    H, Dh, C = MLSTM_HEADS, MLSTM_HEAD_DIM, MLSTM_CHUNK
    T = batch * seq
    nc = seq // C
    assert 4 * nc <= LANES
    c0 = col0 // Dh
    g_row = gates[:, :4 * H].reshape(batch, seq, 4, H).transpose(0, 3, 2, 1)
    g_row = g_row.reshape(batch, H, 4, nc, C)
    gb = gate_b.reshape(4, H).T.reshape(H, 4, 1)
    blk = lambda j: pl.BlockSpec((seq, Dh), lambda b, h: (b, c0 + j * H + h))
    return pl.pallas_call(
        _mlstm_kernel,
        grid=(batch, H),
        in_specs=[blk(0), blk(1), blk(2), blk(3),
                  pl.BlockSpec((CONV_WIDTH, Dh), lambda b, h: (0, h)),
                  pl.BlockSpec((CONV_WIDTH, Dh), lambda b, h: (0, H + h)),
                  pl.BlockSpec((1, 1, 4, nc, C), lambda b, h: (b, h, 0, 0, 0)),
                  pl.BlockSpec((1, 4, 1), lambda b, h: (h, 0, 0))],
        out_specs=pl.BlockSpec((seq, Dh), lambda b, h: (b, h)),
        out_shape=jax.ShapeDtypeStruct((T, H * Dh), BF16),
        scratch_shapes=[pltpu.VMEM((seq, Dh), BF16),
                        pltpu.VMEM((Dh, seq), F32),
                        pltpu.VMEM((Dh, seq), BF16),
                        pltpu.VMEM((seq, Dh + LANES), BF16),
                        pltpu.VMEM((seq, Dh), F32),
                        pltpu.VMEM((Dh, Dh + LANES), F32),
                        pltpu.VMEM((C, LANES), F32)],
        compiler_params=_params("parallel", "parallel"),
        name="bidir_mlstm",
    )(proj, proj, proj, proj, conv_w, conv_w, g_row, gb)


def _router_kernel(x_ref, wr_ref, br_ref, g_ref, rank_ref, cnt_ref):
    tm = x_ref.shape[0]

    @pl.when(pl.program_id(0) == 0)
    def _():
        cnt_ref[...] = jnp.zeros_like(cnt_ref)

    x = x_ref[...]
    x_hi = x.astype(BF16)
    x_lo = (x - x_hi.astype(F32)).astype(BF16)
    w_hi, w_lo = wr_ref[0], wr_ref[1]
    logits = (jnp.dot(x_hi, w_hi, preferred_element_type=F32)
              + (jnp.dot(x_hi, w_lo, preferred_element_type=F32)
                 + jnp.dot(x_lo, w_hi, preferred_element_type=F32))) + br_ref[...]
    lane = lax.broadcasted_iota(jnp.int32, logits.shape, 1)
    lg = jnp.where(lane < N_EXPERTS, logits, -jnp.inf)
    v1 = jnp.max(lg, axis=1, keepdims=True)
    i1 = jnp.min(jnp.where(lg == v1, lane, LANES), axis=1, keepdims=True)
    lg2 = jnp.where(lane == i1, -jnp.inf, lg)
    v2 = jnp.max(lg2, axis=1, keepdims=True)
    i2 = jnp.min(jnp.where(lg2 == v2, lane, LANES), axis=1, keepdims=True)
    e = jnp.exp(v2 - v1)
    p1 = 1.0 / (1.0 + e)
    p2 = e / (1.0 + e)
    g_ref[...] = jnp.where(lane == i1, p1, 0.0) + jnp.where(lane == i2, p2, 0.0)
    sel = (lane == i1) | (lane == i2)
    self_ = jnp.where(sel, 1.0, 0.0)
    row = lax.broadcasted_iota(jnp.int32, (tm, tm), 0)
    col = lax.broadcasted_iota(jnp.int32, (tm, tm), 1)
    earlier = jnp.where(col < row, 1.0, 0.0).astype(BF16)
    raw = (jnp.dot(earlier, self_.astype(BF16), preferred_element_type=F32)
           + cnt_ref[...]).astype(jnp.int32)
    rank_ref[...] = jnp.where(sel, raw, -1 - raw)
    cnt_ref[...] += jnp.sum(self_, axis=0, keepdims=True)


def router(x, wr, br, *, tm):
    T, D = x.shape
    wr_p = jnp.zeros((D, LANES), F32).at[:, :N_EXPERTS].set(wr)
    wr_hi = wr_p.astype(BF16)
    wr_p = jnp.stack([wr_hi, (wr_p - wr_hi.astype(F32)).astype(BF16)])
    br_p = jnp.zeros((1, LANES), F32).at[0, :N_EXPERTS].set(br)
    return pl.pallas_call(
        _router_kernel,
        grid=(T // tm,),
        in_specs=[pl.BlockSpec((tm, D), lambda i: (i, 0)),
                  pl.BlockSpec((2, D, LANES), lambda i: (0, 0, 0)),
                  pl.BlockSpec((1, LANES), lambda i: (0, 0))],
        out_specs=[pl.BlockSpec((tm, LANES), lambda i: (i, 0)),
                   pl.BlockSpec((tm, LANES), lambda i: (i, 0))],
        out_shape=[jax.ShapeDtypeStruct((T, LANES), F32),
                   jax.ShapeDtypeStruct((T, LANES), jnp.int32)],
        scratch_shapes=[pltpu.VMEM((1, LANES), F32)],
        compiler_params=_params("arbitrary"),
        name="moe_router",
    )(x, wr_p, br_p)


MOE_TILE = 512
MOE_TOKEN_BLOCK = 512
MOE_ROW_ALIGN = 16
MOE_STAGE_ROWS = 2 * MOE_TOKEN_BLOCK + N_EXPERTS * MOE_ROW_ALIGN


def _round_up(x, m):
    return (x + m - 1) // m * m


def _moe_plan(rank, gates, *, tile, tb):
    T = rank.shape[0]
    E, A = N_EXPERTS, MOE_ROW_ALIGN
    nb = T // tb
    r = rank[:, :E]
    sel = r >= 0
    raw = jnp.where(sel, r, -1 - r)
    cb = raw[::tb]
    counts = raw[-1] + sel[-1].astype(jnp.int32)
    cnt = jnp.concatenate([cb[1:], counts[None, :]], axis=0) - cb
    padn = _round_up(cnt, A)
    size = jnp.sum(padn, axis=0)
    gsize = _round_up(size, tile)
    ends = jnp.cumsum(gsize)
    start = ends - gsize
    lo = start[None, :] + jnp.cumsum(padn, axis=0) - padn
    off = jnp.cumsum(padn, axis=1) - padn
    within = raw - jnp.repeat(cb, tb, axis=0)
    local = jnp.repeat(off, tb, axis=0) + within
    eid = jnp.arange(E, dtype=jnp.int32)[None, :]
    slot_e = jnp.stack([jnp.min(jnp.where(sel, eid, E - 1), axis=1),
                        jnp.max(jnp.where(sel, eid, 0), axis=1)], axis=1)
    slot_row = jnp.take_along_axis(local, slot_e, axis=1).astype(jnp.int32)
    slot_gate = jnp.take_along_axis(gates[:, :E], slot_e, axis=1)
    n_tiles = -(-(2 * T + nb * E * (A - 1) + E * (tile - 1)) // tile)
    tile_starts = jnp.arange(n_tiles, dtype=jnp.int32) * tile
    count_le = lambda keys, x: jnp.sum((keys[None, :] <= x[:, None]).astype(jnp.int32), axis=1)
    tile_expert = jnp.clip(count_le(ends, tile_starts), 0, E - 1).astype(jnp.int32)
    tile_valid = (tile_starts < ends[-1]).astype(jnp.int32)
    tile_new = jnp.concatenate([jnp.ones((1,), jnp.int32),
                                (tile_expert[1:] != tile_expert[:-1]).astype(jnp.int32)])
    flat = lambda a: a.reshape(-1).astype(jnp.int32)
    ranges = (flat(lo), flat(padn), flat(off))
    tails = ((start + size).astype(jnp.int32), (gsize - size).astype(jnp.int32),
             (ends[-1:] // tile).astype(jnp.int32))
    return slot_row, slot_gate, ranges, tails, (tile_expert, tile_valid, tile_new), n_tiles


def _for_each_piece(n, max_piece, fn):
    o = jnp.int32(0)
    size = max_piece
    while size >= MOE_ROW_ALIGN:
        take = (n & size) != 0

        @pl.when(take)
        def _(o=o, size=size):
            fn(pl.multiple_of(o, MOE_ROW_ALIGN), size)

        o = o + jnp.where(take, size, 0)
        size //= 2


def _gather_kernel(lo_ref, len_ref, off_ref, tail_lo_ref, tail_len_ref, used_tiles_ref,
                   x_ref, rows_ref, xs_hbm, stage, sems):
    b = pl.program_id(0)
    last = pl.num_programs(0) - 1
    slot = b % 2
    tb = x_ref.shape[0]

    def range_copies(blk, slot, act):
        for e in range(N_EXPERTS):
            k = blk * N_EXPERTS + e
            lo = pl.multiple_of(lo_ref[k], MOE_ROW_ALIGN)
            off = pl.multiple_of(off_ref[k], MOE_ROW_ALIGN)
            _for_each_piece(len_ref[k], tb, lambda o, size: act(pltpu.make_async_copy(
                stage.at[slot, pl.ds(off + o, size)], xs_hbm.at[pl.ds(lo + o, size)],
                sems.at[slot])))

    start = lambda c: c.start()
    wait = lambda c: c.wait()

    @pl.when(b >= 2)
    def _():
        range_copies(b - 2, slot, wait)

    rid = lax.broadcasted_iota(jnp.int32, (stage.shape[1], tb), 0)
    rows = rows_ref[...]
    onehot = jnp.where((rows[0:1, :] == rid) | (rows[1:2, :] == rid), 1.0, 0.0).astype(BF16)
    stage[slot] = jnp.dot(onehot, x_ref[...], preferred_element_type=F32).astype(BF16)
    range_copies(b, slot, start)

    @pl.when(b == last)
    def _():
        @pl.when(b >= 1)
        def _():
            range_copies(b - 1, 1 - slot, wait)

        range_copies(b, slot, wait)

        stage[slot, 0:MOE_TILE, :] = jnp.zeros((MOE_TILE, stage.shape[2]), BF16)

        def tail_copies(act):
            for e in range(N_EXPERTS):
                lo = pl.multiple_of(tail_lo_ref[e], MOE_ROW_ALIGN)
                _for_each_piece(tail_len_ref[e], MOE_TILE // 2, lambda o, size: act(
                    pltpu.make_async_copy(stage.at[slot, pl.ds(0, size)],
                                          xs_hbm.at[pl.ds(lo + o, size)], sems.at[slot])))

        tail_copies(start)
        tail_copies(wait)

        def zero_tile(t, carry):
            cp = pltpu.make_async_copy(
                stage.at[slot, pl.ds(0, MOE_TILE)],
                xs_hbm.at[pl.ds(pl.multiple_of(t * MOE_TILE, MOE_TILE), MOE_TILE)], sems.at[slot])
            cp.start()
            cp.wait()
            return carry

        lax.fori_loop(used_tiles_ref[0], xs_hbm.shape[0] // MOE_TILE, zero_tile, 0)


def moe_gather(xb, slot_row_t, ranges, tails, *, n_tiles, tile, tb):
    T, D = xb.shape
    grid_spec = pltpu.PrefetchScalarGridSpec(
        num_scalar_prefetch=6, grid=(T // tb,),
        in_specs=[pl.BlockSpec((tb, D), lambda b, *_: (b, 0)),
                  pl.BlockSpec((2, tb), lambda b, *_: (0, b))],
        out_specs=pl.BlockSpec(memory_space=pl.ANY),
        scratch_shapes=[pltpu.VMEM((2, MOE_STAGE_ROWS, D), BF16), pltpu.SemaphoreType.DMA((2,))])
    return pl.pallas_call(
        _gather_kernel, grid_spec=grid_spec,
        out_shape=jax.ShapeDtypeStruct((n_tiles * tile, D), BF16),
        compiler_params=_params("arbitrary"), name="moe_gather",
    )(*ranges, *tails, xb, slot_row_t)


def _expert_swiglu_kernel(te, tv, tnew, x_ref, w1_ref, w3_ref, o_ref, w1b, w3b):
    i = pl.program_id(1)

    @pl.when(tnew[i] == 1)
    def _():
        w1b[...] = w1_ref[0, 0].astype(BF16)
        w3b[...] = w3_ref[0, 0].astype(BF16)

    @pl.when(tv[i] == 1)
    def _():
        a = x_ref[...]
        h1 = jnp.dot(a, w1b[...], preferred_element_type=F32)
        h3 = jnp.dot(a, w3b[...], preferred_element_type=F32)
        o_ref[...] = (h1 * jax.nn.sigmoid(h1) * h3).astype(o_ref.dtype)

    @pl.when(tv[i] == 0)
    def _():
        o_ref[...] = jnp.zeros_like(o_ref)


def expert_swiglu(xs, w1, w3, layer, tile_expert, tile_valid, tile_new, *, tile, tf):
    R, D = xs.shape
    F = w1.shape[3]
    n_tiles = R // tile
    wspec = pl.BlockSpec((1, 1, D, tf), lambda f, i, te, tv, tn: (layer, te[i], 0, f),
                         pipeline_mode=pl.Buffered(1))
    grid_spec = pltpu.PrefetchScalarGridSpec(
        num_scalar_prefetch=3, grid=(F // tf, n_tiles),
        in_specs=[pl.BlockSpec((tile, D), lambda f, i, te, tv, tn: (i, 0)), wspec, wspec],
        out_specs=pl.BlockSpec((tile, tf), lambda f, i, te, tv, tn: (i, f)),
        scratch_shapes=[pltpu.VMEM((D, tf), BF16), pltpu.VMEM((D, tf), BF16)])
    return pl.pallas_call(
        _expert_swiglu_kernel, grid_spec=grid_spec,
        out_shape=jax.ShapeDtypeStruct((R, F), BF16),
        compiler_params=_params("parallel", "arbitrary"), name="expert_swiglu",
    )(tile_expert, tile_valid, tile_new, xs, w1, w3)


def _expert_w2_kernel(te, tv, tnew, h_ref, w2_ref, o_ref, w2b):
    i = pl.program_id(0)

    @pl.when(tnew[i] == 1)
    def _():
        w2b[...] = w2_ref[0, 0].astype(BF16)

    @pl.when(tv[i] == 1)
    def _():
        o_ref[...] = jnp.dot(h_ref[...], w2b[...], preferred_element_type=F32).astype(o_ref.dtype)

    @pl.when(tv[i] == 0)
    def _():
        o_ref[...] = jnp.zeros_like(o_ref)


def expert_w2(h, w2, layer, tile_expert, tile_valid, tile_new, *, tile):
    R, F = h.shape
    D = w2.shape[3]
    grid_spec = pltpu.PrefetchScalarGridSpec(
        num_scalar_prefetch=3, grid=(R // tile,),
        in_specs=[pl.BlockSpec((tile, F), lambda i, te, tv, tn: (i, 0)),
                  pl.BlockSpec((1, 1, F, D), lambda i, te, tv, tn: (layer, te[i], 0, 0),
                               pipeline_mode=pl.Buffered(1))],
        out_specs=pl.BlockSpec((tile, D), lambda i, te, tv, tn: (i, 0)),
        scratch_shapes=[pltpu.VMEM((F, D), BF16)])
    return pl.pallas_call(
        _expert_w2_kernel, grid_spec=grid_spec,
        out_shape=jax.ShapeDtypeStruct((R, D), BF16),
        compiler_params=_params("arbitrary"), name="expert_w2",
    )(tile_expert, tile_valid, tile_new, h, w2)


def _combine_kernel(lo_ref, len_ref, off_ref, y_hbm, rows_ref, gate_ref, res_ref, lng_ref, lnb_ref,
                    o_ref, ob_ref, stage, sems, *, alpha):
    b = pl.program_id(0)
    slot = b % 2
    tb = rows_ref.shape[0]

    def range_copies(blk, slot, act):
        for e in range(N_EXPERTS):
            k = blk * N_EXPERTS + e
            lo = pl.multiple_of(lo_ref[k], MOE_ROW_ALIGN)
            off = pl.multiple_of(off_ref[k], MOE_ROW_ALIGN)
            _for_each_piece(len_ref[k], tb, lambda o, size: act(pltpu.make_async_copy(
                y_hbm.at[pl.ds(lo + o, size)], stage.at[slot, pl.ds(off + o, size)],
                sems.at[slot])))

    @pl.when(b == 0)
    def _():
        stage[...] = jnp.zeros_like(stage)
        range_copies(0, 0, lambda c: c.start())

    @pl.when(b + 1 < pl.num_programs(0))
    def _():
        range_copies(b + 1, 1 - slot, lambda c: c.start())

    range_copies(b, slot, lambda c: c.wait())

    cid = lax.broadcasted_iota(jnp.int32, (tb, stage.shape[1]), 1)
    rows, gate = rows_ref[...], gate_ref[...]
    weights = (jnp.where(rows[:, 0:1] == cid, gate[:, 0:1], 0.0)
               + jnp.where(rows[:, 1:2] == cid, gate[:, 1:2], 0.0)).astype(BF16)
    f = jnp.dot(weights, stage[slot], preferred_element_type=F32)
    out = _layer_norm_rows(alpha * res_ref[...] + f, lng_ref[...], lnb_ref[...])
    o_ref[...] = out
    ob_ref[...] = out.astype(BF16)


def moe_combine_ln(y, slot_row, slot_gate, res, g, beta, ranges, *, tb, alpha):
    T, D = res.shape
    tok = lambda b, *_: (b, 0)
    const = lambda b, *_: (0, 0)
    grid_spec = pltpu.PrefetchScalarGridSpec(
        num_scalar_prefetch=3, grid=(T // tb,),
        in_specs=[pl.BlockSpec(memory_space=pl.ANY),
                  pl.BlockSpec((tb, 2), tok), pl.BlockSpec((tb, 2), tok),
                  pl.BlockSpec((tb, D), tok),
                  pl.BlockSpec((1, D), const), pl.BlockSpec((1, D), const)],
        out_specs=[pl.BlockSpec((tb, D), tok), pl.BlockSpec((tb, D), tok)],
        scratch_shapes=[pltpu.VMEM((2, MOE_STAGE_ROWS, D), BF16), pltpu.SemaphoreType.DMA((2,))])
    return pl.pallas_call(
        functools.partial(_combine_kernel, alpha=alpha), grid_spec=grid_spec,
        out_shape=[jax.ShapeDtypeStruct((T, D), F32), jax.ShapeDtypeStruct((T, D), BF16)],
        compiler_params=_params("arbitrary"), name="moe_combine_ln",
    )(*ranges, y, slot_row, slot_gate, res, g.reshape(1, D), beta.reshape(1, D))


def moe_layer(xf, xb, wr, br, w1, w3, w2, layer, ln_g, ln_b, *, alpha):
    tile, tb = MOE_TILE, MOE_TOKEN_BLOCK
    gates, rank = router(xf, wr, br, tm=1024)
    slot_row, slot_gate, ranges, tails, tiles, n_tiles = _moe_plan(rank, gates, tile=tile, tb=tb)
    xs = moe_gather(xb, slot_row.T, ranges, tails, n_tiles=n_tiles, tile=tile, tb=tb)
    h = expert_swiglu(xs, w1, w3, layer, *tiles, tile=tile, tf=1408)
    y = expert_w2(h, w2, layer, *tiles, tile=tile)
    return moe_combine_ln(y, slot_row, slot_gate, xf, ln_g, ln_b, ranges, tb=tb, alpha=alpha)


def kernel(x, w_in, w_out, conv_w, gate_b, rpb_table, ln_g, ln_b, dense_w1, dense_w3,
           dense_w2, router_w, router_b, moe_w1, moe_w3, moe_w2):
    B, S, D = x.shape
    T = B * S
    depth = w_in.shape[0]
    alpha = (2 * depth) ** 0.25
    attn_w = ATTN_HEADS * ATTN_HEAD_DIM
    mlstm_w = MLSTM_HEADS * MLSTM_HEAD_DIM
    main_cols = 3 * attn_w + 4 * mlstm_w

    bias_tiles = _attn_bias_tiles(rpb_table)
    xf = x.reshape(T, D)
    xb = xf.astype(BF16)
    w_gate = jnp.pad(lax.slice_in_dim(w_in, main_cols, w_in.shape[2], axis=2),
                     ((0, 0), (0, 0), (0, LANES - 4 * MLSTM_HEADS))).astype(BF16)
    w_in_t = jnp.swapaxes(w_in, 1, 2)
    w_out_b = w_out.astype(BF16)
    dense_w1_b, dense_w3_b, dense_w2_b = (w.astype(BF16) for w in (dense_w1, dense_w3, dense_w2))
    for l in range(depth):
        proj = in_projection(xb, w_in_t, l, n_cols=main_cols, tm=1024, tn=1024)
        gates = matmul(xb, w_gate[l], tm=2048, tn=LANES, out_dtype=F32, name="gate_proj")
        attn = attention(proj, bias_tiles, batch=B, seq=S)
        rec = mlstm(proj, gates, conv_w[l], gate_b[l], batch=B, seq=S, col0=3 * attn_w)
        xf, xb = matmul_residual_ln([attn, rec], w_out_b, l, xf, ln_g[l, 0], ln_b[l, 0],
                                    tm=512, sub=256, alpha=alpha, name="out_proj_ln")
        j = l // 2
        if l % 2 == 0:
            hmid = swiglu_matmul(xb, dense_w1_b, dense_w3_b, j, tm=1024, tf=1408,
                                 name="dense_swiglu")
            xf, xb = matmul_residual_ln([hmid], dense_w2_b, j, xf, ln_g[l, 1], ln_b[l, 1],
                                        tm=256, sub=256, alpha=alpha, name="dense_w2_ln")
        else:
            xf, xb = moe_layer(xf, xb, router_w[j], router_b[j], moe_w1, moe_w3, moe_w2, j,
                               ln_g[l, 1], ln_b[l, 1], alpha=alpha)
    return xf.reshape(B, S, D)
```
